```python
import jax, jax.numpy as jnp
from jax import lax
import numpy as np

D_MODEL = 1024
BATCH = 4
SEQ = 8192
DEPTH = 2
DEC_BATCH = 128
DEC_SEQ = 8
PAST_LEN = 16384
PAGE_SIZE = 128

BRANCH_W = D_MODEL // 2
N_BRANCH = 3
MIX_W = N_BRANCH * BRANCH_W
GLA_HEADS = 4
GLA_DV = BRANCH_W // GLA_HEADS
GLA_DK = GLA_DV // 2
GLA_KW = GLA_HEADS * GLA_DK
GLA_VW = GLA_HEADS * GLA_DV
GATE_RANK = 16
GATE_NORMALIZER = 16.0
GLA_CHUNK = 64
CONV_C = BRANCH_W
CONV_K = 31
SWA_HQ = 8
SWA_HKV = 2
SWA_G = SWA_HQ // SWA_HKV
SWA_DH = BRANCH_W // SWA_HQ
WINDOW = 128
N_GROUPS = 4
EXPERTS_PER_GROUP = 8
N_EXPERTS = N_GROUPS * EXPERTS_PER_GROUP
TOP_K = 2
EXPERT_F = D_MODEL // 4
EPS = 1e-6

IN_SIZES = (GLA_KW, GLA_KW, GLA_VW, GLA_VW, GATE_RANK,
            2 * CONV_C,
            SWA_HQ * SWA_DH, SWA_HKV * SWA_DH, SWA_HKV * SWA_DH,
            N_BRANCH * D_MODEL)
IN_COLS = sum(IN_SIZES)

kernel_name = "hybrid_gla_conformer_swa_hiermoe_step"


def _split_points():
    pts, acc = [], 0
    for s in IN_SIZES[:-1]:
        acc += s
        pts.append(acc)
    return pts


def _rmsnorm(x, g):
    xf = x.astype(jnp.float32)
    r = lax.rsqrt(jnp.mean(xf * xf, -1, keepdims=True) + EPS)
    return (xf * r).astype(x.dtype) * g


def _layer_norm(x, g, b):
    xf = x.astype(jnp.float32)
    mu = jnp.mean(xf, -1, keepdims=True)
    var = jnp.mean(jnp.square(xf - mu), -1, keepdims=True)
    return ((xf - mu) * lax.rsqrt(var + EPS)).astype(x.dtype) * g + b


def _alibi_slopes():
    return jnp.exp2(-8.0 * jnp.arange(1, SWA_HQ + 1, dtype=jnp.float32) / SWA_HQ)


def _gla(q, k, v, log_a, s0):
    B, L, H, _ = q.shape
    C = min(GLA_CHUNK, L)
    n = -(-L // C)
    pad = n * C - L
    if pad:
        pw = ((0, 0), (0, pad), (0, 0), (0, 0))
        q, k, v, log_a = [jnp.pad(t, pw) for t in (q, k, v, log_a)]

    def to_chunks(t):
        return t.astype(jnp.float32).reshape(B, n, C, H, t.shape[-1]).transpose(1, 0, 3, 2, 4)

    qc, kc, vc, ac = (to_chunks(t) for t in (q, k, v, log_a))
    causal = jnp.tril(jnp.ones((C, C), dtype=bool))

    def step(S, inp):
        qi, ki, vi, ai = inp
        b = jnp.cumsum(ai, axis=2)
        diff = b[:, :, :, None, :] - b[:, :, None, :, :]
        decay = jnp.exp(jnp.where(causal[:, :, None], diff, -jnp.inf))
        attn = jnp.einsum('bhtd,bhsd,bhtsd->bhts', qi, ki, decay)
        o = (jnp.einsum('bhtd,bhdv->bhtv', qi * jnp.exp(b), S)
             + jnp.einsum('bhts,bhsv->bhtv', attn, vi))
        b_last = b[:, :, -1:, :]
        S_new = (jnp.exp(b_last[:, :, 0, :, None]) * S
                 + jnp.einsum('bhsd,bhsv->bhdv', ki * jnp.exp(b_last - b), vi))
        return S_new, o

    S, o = lax.scan(step, s0.astype(jnp.float32), (qc, kc, vc, ac))
    o = o.transpose(1, 0, 3, 2, 4).reshape(B, n * C, H, GLA_DV)[:, :L]
    return o, S


def _dwconv_causal(u_ext, w, b):
    out = lax.conv_general_dilated(u_ext, w[:, None, :].astype(u_ext.dtype), window_strides=(1,),
                                   padding='VALID', dimension_numbers=('NWC', 'WIO', 'NWC'),
                                   feature_group_count=u_ext.shape[-1])
    return out + b


def _sink_attend(q, k, v, q_pos, k_pos, sinks):
    f32 = jnp.float32
    s = jnp.einsum('bnqhgd,bnkhd->bnhgqk', q.astype(f32), k.astype(f32)) * (SWA_DH ** -0.5)
    dist = q_pos[:, :, None] - k_pos[:, None, :]
    allowed = (dist >= 0) & (dist <= WINDOW) & (k_pos[:, None, :] >= 0)
    slopes = _alibi_slopes().reshape(SWA_HKV, SWA_G)[None, None, :, :, None, None]
    s = s - slopes * dist.astype(f32)[None, :, None, None, :, :]
    s = jnp.where(allowed[None, :, None, None, :, :], s, -jnp.inf)
    sink = sinks.astype(f32).reshape(SWA_HKV, SWA_G)[None, None, :, :, None, None]
    m = jnp.maximum(jnp.max(s, -1, keepdims=True), sink)
    p = jnp.exp(s - m)
    p = p / (jnp.sum(p, -1, keepdims=True) + jnp.exp(sink - m))
    o = jnp.einsum('bnhgqk,bnkhd->bnqhgd', p, v.astype(f32))
    return o.astype(q.dtype)


def _swa_banded(q, k, v, sinks):
    B, L = q.shape[:2]
    nb = L // WINDOW
    qb = q.reshape(B, nb, WINDOW, SWA_HKV, SWA_G, SWA_DH)
    kb = k.reshape(B, nb, WINDOW, SWA_HKV, SWA_DH)
    vb = v.reshape(B, nb, WINDOW, SWA_HKV, SWA_DH)
    pw = ((0, 0), (1, 0), (0, 0), (0, 0), (0, 0))
    kx = jnp.concatenate([jnp.pad(kb, pw)[:, :-1], kb], axis=2)
    vx = jnp.concatenate([jnp.pad(vb, pw)[:, :-1], vb], axis=2)
    blk = jnp.arange(nb, dtype=jnp.int32)[:, None] * WINDOW
    q_pos = blk + jnp.arange(WINDOW, dtype=jnp.int32)[None]
    k_pos = blk - WINDOW + jnp.arange(2 * WINDOW, dtype=jnp.int32)[None]
    o = _sink_attend(qb, kx, vx, q_pos, k_pos, sinks)
    return o.reshape(B, L, SWA_HQ * SWA_DH)


def _hier_moe(h, p):
    B, L, D = h.shape
    hf = h.reshape(B * L, D)
    g_logits = (hf @ p['w_router_group'] + p['b_router_group']).astype(jnp.float32)
    g_sel = jnp.argmax(g_logits, -1)
    g_prob = jnp.max(jax.nn.softmax(g_logits, -1), -1)
    e_logits = (hf @ p['w_router_expert'] + p['b_router_expert']).astype(jnp.float32)
    e_logits = e_logits.reshape(-1, N_GROUPS, EXPERTS_PER_GROUP)
    e_in_group = jnp.einsum('tge,tg->te', e_logits, jax.nn.one_hot(g_sel, N_GROUPS, dtype=jnp.float32))
    top_v, top_i = lax.top_k(e_in_group, TOP_K)
    w_sel = jax.nn.softmax(top_v, -1) * g_prob[:, None]
    e_idx = g_sel[:, None] * EXPERTS_PER_GROUP + top_i
    combine = jnp.einsum('tk,tke->te', w_sel, jax.nn.one_hot(e_idx, N_EXPERTS, dtype=jnp.float32))
    combine = combine.astype(hf.dtype)
    out = jnp.zeros_like(hf)
    for gi in range(N_GROUPS):
        sl = slice(gi * EXPERTS_PER_GROUP, (gi + 1) * EXPERTS_PER_GROUP)
        a = jnp.einsum('td,edf->tef', hf, p['w_exp_gate'][sl])
        u = jnp.einsum('td,edf->tef', hf, p['w_exp_up'][sl])
        hid = jax.nn.silu(a) * u * combine[:, sl, None]
        out = out + jnp.einsum('tef,efd->td', hid, p['w_exp_down'][sl])
    return out.reshape(B, L, D)


def _trunk_layer(x, p, gla_s0, conv_buf, swa_kbuf, swa_vbuf, pos0):
    B, L, _ = x.shape
    h = _rmsnorm(x, p['norm_mix_g'])
    proj = h @ p['w_in']
    gq, gk, gv, gg, gfr, cglu, sq, sk, sv, mg = jnp.split(proj, _split_points(), axis=-1)

    log_a = jax.nn.log_sigmoid((gfr @ p['w_fgate2'] + p['b_fgate2']).astype(jnp.float32)) / GATE_NORMALIZER
    o_gla, s_new = _gla(gq.reshape(B, L, GLA_HEADS, GLA_DK) * (GLA_DK ** -0.5),
                        gk.reshape(B, L, GLA_HEADS, GLA_DK),
                        gv.reshape(B, L, GLA_HEADS, GLA_DV),
                        log_a.reshape(B, L, GLA_HEADS, GLA_DK), gla_s0)
    o_gla = _rmsnorm(o_gla.astype(x.dtype), p['gla_norm_g']) * jax.nn.silu(gg.reshape(B, L, GLA_HEADS, GLA_DV))
    o_gla = o_gla.reshape(B, L, GLA_VW)

    ca, cg = jnp.split(cglu, 2, axis=-1)
    u = ca * jax.nn.sigmoid(cg)
    u_ext = jnp.concatenate([conv_buf.astype(u.dtype), u], axis=1)
    c = _dwconv_causal(u_ext, p['conv_w'], p['conv_b'])
    o_conv = jax.nn.silu(_layer_norm(c, p['conv_ln_g'], p['conv_ln_b']))
    conv_new = u_ext[:, -(CONV_K - 1):]

    q = sq.reshape(B, L, SWA_HKV, SWA_G, SWA_DH)
    k = sk.reshape(B, L, SWA_HKV, SWA_DH)
    v = sv.reshape(B, L, SWA_HKV, SWA_DH)
    if swa_kbuf is None:
        o_swa = _swa_banded(q, k, v, p['swa_sinks'])
        keep = min(WINDOW, L)
        k_new, v_new = k[:, -keep:], v[:, -keep:]
    else:
        nbuf = swa_kbuf.shape[1]
        k_ext = jnp.concatenate([swa_kbuf.astype(k.dtype), k], axis=1)
        v_ext = jnp.concatenate([swa_vbuf.astype(v.dtype), v], axis=1)
        q_pos = pos0 + jnp.arange(L, dtype=jnp.int32)
        k_pos = pos0 - nbuf + jnp.arange(nbuf + L, dtype=jnp.int32)
        o_swa = _sink_attend(q[:, None], k_ext[:, None], v_ext[:, None], q_pos[None], k_pos[None],
                             p['swa_sinks'])[:, 0].reshape(B, L, SWA_HQ * SWA_DH)
        k_new, v_new = k_ext[:, -nbuf:], v_ext[:, -nbuf:]

    branches = jnp.stack([o_gla, o_conv, o_swa], axis=2)
    y_b = jnp.einsum('blnc,ncd->blnd', branches, p['w_branch'].reshape(N_BRANCH, BRANCH_W, D_MODEL))
    gates = jax.nn.sigmoid(mg.reshape(B, L, N_BRANCH, D_MODEL))
    y = jnp.sum(gates * y_b, axis=2)
    x = x + y @ p['w_out']

    x = x + _hier_moe(_rmsnorm(x, p['norm_ffn_g']), p)
    return x, s_new.astype(x.dtype), conv_new, k_new, v_new


def setup_inputs(seed: int = 0) -> dict:
    key = jax.random.key(seed)
    ks = iter(jax.random.split(key, 40))
    f32 = jnp.float32
    nrm = lambda shape, scale: jax.random.normal(next(ks), shape, f32) * scale
    n_buf = min(WINDOW, PAST_LEN)
    return {
        'x_prompt': nrm((BATCH, SEQ, D_MODEL), 1.0),
        'x_sample': nrm((DEC_BATCH, DEC_SEQ, D_MODEL), 1.0),
        'state_gla': nrm((DEPTH, DEC_BATCH, GLA_HEADS, GLA_DK, GLA_DV), 0.5),
        'cache_conv': nrm((DEPTH, DEC_BATCH, CONV_K - 1, CONV_C), 0.5),
        'cache_swa_k': nrm((DEPTH, DEC_BATCH, n_buf, SWA_HKV, SWA_DH), 1.0),
        'cache_swa_v': nrm((DEPTH, DEC_BATCH, n_buf, SWA_HKV, SWA_DH), 1.0),
        'norm_mix_g': 1.0 + nrm((DEPTH, D_MODEL), 0.02),
        'w_in': nrm((DEPTH, D_MODEL, IN_COLS), D_MODEL ** -0.5),
        'w_fgate2': nrm((DEPTH, GATE_RANK, GLA_KW), GATE_RANK ** -0.5),
        'b_fgate2': nrm((DEPTH, GLA_KW), 0.1),
        'gla_norm_g': 1.0 + nrm((DEPTH, GLA_DV), 0.02),
        'conv_w': nrm((DEPTH, CONV_K, CONV_C), CONV_K ** -0.5),
        'conv_b': nrm((DEPTH, CONV_C), 0.02),
        'conv_ln_g': 1.0 + nrm((DEPTH, CONV_C), 0.02),
        'conv_ln_b': nrm((DEPTH, CONV_C), 0.02),
        'swa_sinks': nrm((DEPTH, SWA_HQ), 0.5),
        'w_branch': nrm((DEPTH, MIX_W, D_MODEL), BRANCH_W ** -0.5),
        'w_out': nrm((DEPTH, D_MODEL, D_MODEL), D_MODEL ** -0.5),
        'norm_ffn_g': 1.0 + nrm((DEPTH, D_MODEL), 0.02),
        'w_router_group': nrm((DEPTH, D_MODEL, N_GROUPS), D_MODEL ** -0.5),
        'b_router_group': nrm((DEPTH, N_GROUPS), 0.01),
        'w_router_expert': nrm((DEPTH, D_MODEL, N_EXPERTS), D_MODEL ** -0.5),
        'b_router_expert': nrm((DEPTH, N_EXPERTS), 0.01),
        'w_exp_gate': nrm((DEPTH, N_EXPERTS, D_MODEL, EXPERT_F), D_MODEL ** -0.5),
        'w_exp_up': nrm((DEPTH, N_EXPERTS, D_MODEL, EXPERT_F), D_MODEL ** -0.5),
        'w_exp_down': nrm((DEPTH, N_EXPERTS, EXPERT_F, D_MODEL), EXPERT_F ** -0.5),
        'final_norm_g': 1.0 + nrm((D_MODEL,), 0.02),
    }


def reference(x_prompt, x_sample, state_gla, cache_conv, cache_swa_k, cache_swa_v,
              norm_mix_g, w_in, w_fgate2, b_fgate2, gla_norm_g, conv_w, conv_b, conv_ln_g, conv_ln_b,
              swa_sinks, w_branch, w_out, norm_ffn_g, w_router_group, b_router_group,
              w_router_expert, b_router_expert, w_exp_gate, w_exp_up, w_exp_down, final_norm_g):
    xp, xs = x_prompt, x_sample
    gla_p, gla_s, conv_p, conv_s, kp, ks_, vp, vs = [], [], [], [], [], [], [], []
    for l in range(DEPTH):
        p = {'norm_mix_g': norm_mix_g[l], 'w_in': w_in[l], 'w_fgate2': w_fgate2[l], 'b_fgate2': b_fgate2[l],
             'gla_norm_g': gla_norm_g[l], 'conv_w': conv_w[l], 'conv_b': conv_b[l],
             'conv_ln_g': conv_ln_g[l], 'conv_ln_b': conv_ln_b[l], 'swa_sinks': swa_sinks[l],
             'w_branch': w_branch[l], 'w_out': w_out[l], 'norm_ffn_g': norm_ffn_g[l],
             'w_router_group': w_router_group[l], 'b_router_group': b_router_group[l],
             'w_router_expert': w_router_expert[l], 'b_router_expert': b_router_expert[l],
             'w_exp_gate': w_exp_gate[l], 'w_exp_up': w_exp_up[l], 'w_exp_down': w_exp_down[l]}
        s0 = jnp.zeros((xp.shape[0], GLA_HEADS, GLA_DK, GLA_DV), jnp.float32)
        cb0 = jnp.zeros((xp.shape[0], CONV_K - 1, CONV_C), xp.dtype)
        xp, a1, a2, a3, a4 = _trunk_layer(xp, p, s0, cb0, None, None, 0)
        gla_p.append(a1); conv_p.append(a2); kp.append(a3); vp.append(a4)
        xs, b1, b2, b3, b4 = _trunk_layer(xs, p, state_gla[l], cache_conv[l], cache_swa_k[l], cache_swa_v[l],
                                          PAST_LEN)
        gla_s.append(b1); conv_s.append(b2); ks_.append(b3); vs.append(b4)
    y_prompt = _rmsnorm(xp, final_norm_g)
    y_sample = _rmsnorm(xs, final_norm_g)
    return (y_prompt, y_sample, jnp.stack(gla_p), jnp.stack(gla_s), jnp.stack(conv_p), jnp.stack(conv_s),
            jnp.stack(kp), jnp.stack(ks_), jnp.stack(vp), jnp.stack(vs))
```

```python
import functools

import numpy as np
import jax
import jax.numpy as jnp
from jax import lax
from jax.experimental import pallas as pl
from jax.experimental.pallas import tpu as pltpu

F32 = jnp.float32
BF16 = jnp.bfloat16

D_MODEL = 1024
BRANCH_W = 512
GLA_HEADS = 4
GLA_DK = 64
GLA_DV = 128
GLA_KW = GLA_HEADS * GLA_DK
GLA_VW = GLA_HEADS * GLA_DV
GATE_RANK = 16
GATE_NORMALIZER = 16.0
GLA_CHUNK = 64
CONV_C = 512
CONV_K = 31
SWA_HQ = 8
SWA_HKV = 2
SWA_G = SWA_HQ // SWA_HKV
SWA_DH = 64
WINDOW = 128
N_GROUPS = 4
EXPERTS_PER_GROUP = 8
N_EXPERTS = 32
EXPERT_F = 256
EPS = 1e-6
PAST_LEN = 16384

LANES = 128
VMEM_LIMIT_BYTES = 56 * 2**20
ROUTER_LANES = LANES
EXPERT_LANE0 = N_GROUPS
CONV_HIST = 32


def _mm(a, b):
    return jnp.dot(a, b, preferred_element_type=F32)


def _mm_nt(a, b):
    return lax.dot_general(a, b, (((1,), (1,)), ((), ())), preferred_element_type=F32)


def _mm_tn(a, b):
    return lax.dot_general(a, b, (((0,), (0,)), ((), ())), preferred_element_type=F32)


def _split2(x):
    hi = x.astype(BF16)
    lo = (x - hi.astype(F32)).astype(BF16)
    return hi, lo


def _split3(x):
    hi = x.astype(BF16)
    r = x - hi.astype(F32)
    mid = r.astype(BF16)
    lo = (r - mid.astype(F32)).astype(BF16)
    return hi, mid, lo


def _const_spec(shape):
    nd = len(shape)
    return pl.BlockSpec(shape, lambda *_: (0,) * nd)


def _params(*sem):
    return pltpu.CompilerParams(dimension_semantics=sem, vmem_limit_bytes=VMEM_LIMIT_BYTES)


def _rms(x):
    return lax.rsqrt(jnp.mean(x * x, axis=-1, keepdims=True) + EPS)


def _log_sigmoid(z):
    return jnp.minimum(z, 0.0) - jnp.log1p(jnp.exp(-jnp.abs(z)))


def _in_proj_body(x_ref, g_ref, wqkvg_ref, wfr_ref, wf2_ref, bf2_ref, wcglu_ref, wswa_ref, wmg_ref,
                  qkvg_ref, la_ref, u_ref, sq_ref, sk_ref, sv_ref, gates_ref):
    x = x_ref[...]
    h = ((x * _rms(x)) * g_ref[...]).astype(BF16)
    qkvg_ref[...] = _mm(h, wqkvg_ref[...])
    gfr_hi, gfr_lo = _split2(_mm(h, wfr_ref[...]))
    z = _mm(gfr_hi, wf2_ref[0]) + (_mm(gfr_lo, wf2_ref[0]) + _mm(gfr_hi, wf2_ref[1])) + bf2_ref[...]
    la_ref[...] = _log_sigmoid(z) * (1.0 / GATE_NORMALIZER)
    cglu = _mm(h, wcglu_ref[...])
    u_ref[...] = cglu[:, :CONV_C] * jax.nn.sigmoid(cglu[:, CONV_C:])
    swa = _mm(h, wswa_ref[...])
    sq_ref[...] = swa[:, :SWA_HQ * SWA_DH] * (SWA_DH ** -0.5)
    sk_ref[...] = swa[:, SWA_HQ * SWA_DH:SWA_HQ * SWA_DH + SWA_HKV * SWA_DH]
    sv_ref[...] = swa[:, SWA_HQ * SWA_DH + SWA_HKV * SWA_DH:]
    gates_ref[...] = jax.nn.sigmoid(_mm(h, wmg_ref[...])).astype(BF16)


def _in_proj(x, lw, tm):
    T = x.shape[0]
    row = lambda n: pl.BlockSpec((tm, n), lambda i: (i, 0))
    outs = [(GLA_KW * 2 + GLA_VW * 2, F32), (GLA_KW, F32), (CONV_C, F32), (SWA_HQ * SWA_DH, F32),
            (SWA_HKV * SWA_DH, F32), (SWA_HKV * SWA_DH, F32), (3 * D_MODEL, BF16)]
    consts = [lw['norm_mix_g'], lw['w_qkvg'], lw['w_fr'], lw['w_f2'], lw['b_f2'], lw['w_cglu'], lw['w_swa'],
              lw['w_mg']]
    return pl.pallas_call(
        _in_proj_body,
        grid=(T // tm,),
        in_specs=[row(D_MODEL)] + [_const_spec(c.shape) for c in consts],
        out_specs=[row(n) for n, _ in outs],
        out_shape=[jax.ShapeDtypeStruct((T, n), dt) for n, dt in outs],
        compiler_params=_params("parallel"),
        name="in_proj",
    )(x, *consts)


def _gla_levels(C):
    return [w for w in (32, 16, 8, 4, 2, 1) if w < C]


def _gla_sum_matrix(C):
    t = np.arange(C)
    mats = [(t[:, None] >= t[None, :]).astype(np.float32)]
    qs, ks = [], []
    for w in _gla_levels(C):
        start = (t // (2 * w)) * (2 * w)
        upper = (t % (2 * w)) >= w
        mq = upper[:, None] & (t[None, :] >= (start + w)[:, None]) & (t[None, :] <= t[:, None])
        mk = (~upper)[:, None] & (t[None, :] > t[:, None]) & (t[None, :] <= (start + w - 1)[:, None])
        qs.append(mq.astype(np.float32))
        ks.append(mk.astype(np.float32))
    return np.concatenate(mats + qs + ks, axis=0)


def _gla_body(qkvg_ref, la_ref, s0_ref, m_ref, gng_ref, o_ref, s_out_ref, st_ref, *, C, TT, exact):
    levels = _gla_levels(C)
    n = len(levels)
    mmdt = F32 if exact else BF16
    step = pl.program_id(1)

    @pl.when(step == 0)
    def _():
        for h in range(GLA_HEADS):
            st_ref[h] = s0_ref[h].T

    def chunk(c, carry):
        rows = pl.ds(pl.multiple_of(c * C, C), C)
        qkvg = qkvg_ref[rows, :]
        la = la_ref[rows, :]
        if exact:
            e_all = jnp.dot(m_ref[...], la, precision=lax.Precision.HIGHEST, preferred_element_type=F32)
        else:
            e_all = _mm(m_ref[...], jnp.concatenate(_split3(la), axis=0))
        b = e_all[0:C]
        q = qkvg[:, 0:GLA_KW] * (GLA_DK ** -0.5)
        k = qkvg[:, GLA_KW:2 * GLA_KW]
        v = qkvg[:, 2 * GLA_KW:2 * GLA_KW + GLA_VW]
        g = qkvg[:, 2 * GLA_KW + GLA_VW:]
        rowi = lax.broadcasted_iota(jnp.int32, (C, GLA_KW), 0)
        qs, ks = [], []
        for li, w in enumerate(levels):
            eq = e_all[(1 + li) * C:(2 + li) * C]
            ek = e_all[(1 + n + li) * C:(2 + n + li) * C]
            upper = (rowi & w) != 0
            qs.append(jnp.where(upper, q * jnp.exp(eq), 0.0).astype(mmdt))
            ks.append(jnp.where(upper, 0.0, k * jnp.exp(ek)).astype(mmdt))
        qb = (q * jnp.exp(b)).astype(mmdt)
        b_last = b[C - 1:C, :]
        kd = (k * jnp.exp(b_last - b)).astype(mmdt)
        dec = jnp.exp(b_last)
        qm = q.astype(mmdt)
        km = k.astype(mmdt)
        ti = lax.broadcasted_iota(jnp.int32, (C, C), 0)
        si = lax.broadcasted_iota(jnp.int32, (C, C), 1)
        for h in range(GLA_HEADS):
            hl = slice(h * GLA_DK, (h + 1) * GLA_DK)
            vl = slice(h * GLA_DV, (h + 1) * GLA_DV)
            attn = jnp.where(ti == si, _mm_nt(qm[:, hl], km[:, hl]), 0.0)
            for li, w in enumerate(levels):
                p = _mm_nt(qs[li][:, hl], ks[li][:, hl])
                if 2 * w < C:
                    sh = (2 * w).bit_length() - 1
                    p = jnp.where(lax.shift_right_logical(ti, sh) == lax.shift_right_logical(si, sh), p, 0.0)
                attn = attn + p
            vh = v[:, vl].astype(mmdt)
            st = st_ref[h]
            o = _mm(attn.astype(mmdt), vh) + _mm_nt(qb[:, hl], st.astype(mmdt))
            st_ref[h] = st * dec[:, hl] + _mm_tn(vh, kd[:, hl])
            gh = g[:, vl]
            o = ((o * _rms(o)) * gng_ref[...]) * (gh * jax.nn.sigmoid(gh))
            o_ref[rows, vl] = o.astype(o_ref.dtype)
        return carry

    if TT == C:
        chunk(0, 0)
    else:
        lax.fori_loop(0, TT // C, chunk, 0)

    @pl.when(step == pl.num_programs(1) - 1)
    def _():
        for h in range(GLA_HEADS):
            s_out_ref[h] = st_ref[h].T


def _gla(qkvg, la, s0, gng, TT):
    B, L, _ = qkvg.shape
    C = min(GLA_CHUNK, L)
    TT = min(TT, L)
    exact = C < 16
    m = _gla_sum_matrix(C)
    m = jnp.asarray(m, F32) if exact else jnp.asarray(np.concatenate([m, m, m], axis=1), BF16)
    body = functools.partial(_gla_body, C=C, TT=TT, exact=exact)
    return pl.pallas_call(
        body,
        grid=(B, L // TT),
        in_specs=[pl.BlockSpec((None, TT, qkvg.shape[2]), lambda b, i: (b, i, 0)),
                  pl.BlockSpec((None, TT, GLA_KW), lambda b, i: (b, i, 0)),
                  pl.BlockSpec((None, GLA_HEADS, GLA_DK, GLA_DV), lambda b, i: (b, 0, 0, 0)),
                  _const_spec(m.shape), _const_spec(gng.shape)],
        out_specs=[pl.BlockSpec((None, TT, GLA_VW), lambda b, i: (b, i, 0)),
                   pl.BlockSpec((None, GLA_HEADS, GLA_DK, GLA_DV), lambda b, i: (b, 0, 0, 0))],
        out_shape=[jax.ShapeDtypeStruct((B, L, GLA_VW), BF16),
                   jax.ShapeDtypeStruct((B, GLA_HEADS, GLA_DK, GLA_DV), F32)],
        scratch_shapes=[pltpu.VMEM((GLA_HEADS, GLA_DV, GLA_DK), F32)],
        compiler_params=_params("parallel", "arbitrary"),
        name="gla",
    )(qkvg, la, s0, m, gng)


def _conv_body(u_ref, buf_ref, w_ref, cb_ref, lg_ref, lb_ref, o_ref, new_ref, ext_ref, *, TT):
    lo = CONV_HIST - (CONV_K - 1)

    @pl.when(pl.program_id(1) == 0)
    def _():
        ext_ref[lo:CONV_HIST, :] = buf_ref[...]

    ext_ref[CONV_HIST:CONV_HIST + TT, :] = u_ref[...]
    acc = jnp.zeros((TT, CONV_C), F32) + cb_ref[...]
    for j in range(CONV_K):
        acc = acc + w_ref[j:j + 1, :] * ext_ref[lo + j:lo + j + TT, :]
    mu = jnp.mean(acc, axis=-1, keepdims=True)
    d = acc - mu
    var = jnp.mean(d * d, axis=-1, keepdims=True)
    y = (d * lax.rsqrt(var + EPS)) * lg_ref[...] + lb_ref[...]
    o_ref[...] = (y * jax.nn.sigmoid(y)).astype(o_ref.dtype)
    tail = ext_ref[TT + lo:TT + CONV_HIST, :]
    new_ref[...] = tail
    ext_ref[lo:CONV_HIST, :] = tail


def _conv(u, buf, w, cb, lg, lb, TT):
    B, L, _ = u.shape
    TT = min(TT, L)
    body = functools.partial(_conv_body, TT=TT)
    return pl.pallas_call(
        body,
        grid=(B, L // TT),
        in_specs=[pl.BlockSpec((None, TT, CONV_C), lambda b, i: (b, i, 0)),
                  pl.BlockSpec((None, CONV_K - 1, CONV_C), lambda b, i: (b, 0, 0)),
                  _const_spec(w.shape), _const_spec(cb.shape), _const_spec(lg.shape), _const_spec(lb.shape)],
        out_specs=[pl.BlockSpec((None, TT, CONV_C), lambda b, i: (b, i, 0)),
                   pl.BlockSpec((None, CONV_K - 1, CONV_C), lambda b, i: (b, 0, 0))],
        out_shape=[jax.ShapeDtypeStruct((B, L, CONV_C), BF16),
                   jax.ShapeDtypeStruct((B, CONV_K - 1, CONV_C), F32)],
        scratch_shapes=[pltpu.VMEM((CONV_HIST + TT, CONV_C), F32)],
        compiler_params=_params("parallel", "arbitrary"),
        name="conv",
    )(u, buf, w, cb, lg, lb)


def _swa_body(q_ref, kp_ref, kc_ref, vp_ref, vc_ref, sink_ref, o_ref, *, TQ, first_has_history, mmdt):
    q = q_ref[...].astype(mmdt)
    kp = kp_ref[...].astype(mmdt)
    kc = kc_ref[...].astype(mmdt)
    vp = vp_ref[...].astype(mmdt)
    vc = vc_ref[...].astype(mmdt)
    i1 = lax.broadcasted_iota(jnp.int32, (TQ, WINDOW), 0)
    j1 = lax.broadcasted_iota(jnp.int32, (TQ, WINDOW), 1)
    ok1 = j1 >= i1
    if not first_has_history:
        ok1 = jnp.logical_and(ok1, pl.program_id(1) > 0)
    d1 = (WINDOW + i1 - j1).astype(F32)
    i2 = lax.broadcasted_iota(jnp.int32, (TQ, TQ), 0)
    j2 = lax.broadcasted_iota(jnp.int32, (TQ, TQ), 1)
    ok2 = j2 <= i2
    d2 = (i2 - j2).astype(F32)
    for h in range(SWA_HQ):
        ql = slice(h * SWA_DH, (h + 1) * SWA_DH)
        kl = slice((h // SWA_G) * SWA_DH, (h // SWA_G + 1) * SWA_DH)
        slope = 2.0 ** (-8.0 * (h + 1) / SWA_HQ)
        s1 = jnp.where(ok1, _mm_nt(q[:, ql], kp[:, kl]) - slope * d1, -jnp.inf)
        s2 = jnp.where(ok2, _mm_nt(q[:, ql], kc[:, kl]) - slope * d2, -jnp.inf)
        sink = sink_ref[h]
        m = jnp.maximum(jnp.maximum(jnp.max(s1, -1, keepdims=True), jnp.max(s2, -1, keepdims=True)), sink)
        p1 = jnp.exp(s1 - m)
        p2 = jnp.exp(s2 - m)
        den = jnp.sum(p1, -1, keepdims=True) + jnp.sum(p2, -1, keepdims=True) + jnp.exp(sink - m)
        o = (_mm(p1.astype(mmdt), vp[:, kl]) + _mm(p2.astype(mmdt), vc[:, kl])) / den
        o_ref[:, ql] = o.astype(o_ref.dtype)


def _swa(q, k, v, kprev, vprev, sinks):
    B, L, _ = q.shape
    TQ = min(WINDOW, L)
    kvw = SWA_HKV * SWA_DH
    cur = pl.BlockSpec((None, TQ, kvw), lambda b, n: (b, n, 0))
    if kprev is None:
        prev = pl.BlockSpec((None, WINDOW, kvw), lambda b, n: (b, jnp.maximum(n - 1, 0), 0))
        kprev, vprev = k, v
    else:
        assert L == TQ
        prev = pl.BlockSpec((None, WINDOW, kvw), lambda b, n: (b, 0, 0))
    body = functools.partial(_swa_body, TQ=TQ, first_has_history=kprev is not k, mmdt=BF16 if TQ >= 16 else F32)
    return pl.pallas_call(
        body,
        grid=(B, L // TQ),
        in_specs=[pl.BlockSpec((None, TQ, SWA_HQ * SWA_DH), lambda b, n: (b, n, 0)),
                  prev, cur, prev, cur, pl.BlockSpec(memory_space=pltpu.SMEM)],
        out_specs=pl.BlockSpec((None, TQ, SWA_HQ * SWA_DH), lambda b, n: (b, n, 0)),
        out_shape=jax.ShapeDtypeStruct((B, L, SWA_HQ * SWA_DH), BF16),
        compiler_params=_params("parallel", "arbitrary"),
        name="swa",
    )(q, kprev, k, vprev, v, sinks)


def _route(logits):
    lane = lax.broadcasted_iota(jnp.int32, logits.shape, 1)
    big = jnp.int32(2 * ROUTER_LANES)
    is_g = lane < N_GROUPS
    gl = jnp.where(is_g, logits, -jnp.inf)
    gmax = jnp.max(gl, -1, keepdims=True)
    gsel = jnp.min(jnp.where(gl == gmax, lane, big), -1, keepdims=True)
    gprob = 1.0 / jnp.sum(jnp.where(is_g, jnp.exp(logits - gmax), 0.0), -1, keepdims=True)
    e = lane - EXPERT_LANE0
    in_grp = jnp.logical_and(jnp.logical_and(e >= 0, e < N_EXPERTS),
                             lax.shift_right_logical(jnp.maximum(e, 0), 3) == gsel)
    el = jnp.where(in_grp, logits, -jnp.inf)
    v1 = jnp.max(el, -1, keepdims=True)
    i1 = jnp.min(jnp.where(el == v1, lane, big), -1, keepdims=True)
    el2 = jnp.where(lane == i1, -jnp.inf, el)
    v2 = jnp.max(el2, -1, keepdims=True)
    i2 = jnp.min(jnp.where(el2 == v2, lane, big), -1, keepdims=True)
    t = jnp.exp(v2 - v1)
    w1 = gprob / (1.0 + t)
    w2 = gprob * t / (1.0 + t)
    return jnp.where(lane == i1, w1, 0.0) + jnp.where(lane == i2, w2, 0.0)


def _merge_body(og_ref, oc_ref, os_ref, gates_ref, x_ref, wb_ref, wo_ref, g2_ref, wr_ref, br_ref,
                x1_ref, h2_ref, comb_ref):
    y = (gates_ref[:, 0:D_MODEL].astype(F32) * _mm(og_ref[...], wb_ref[0])
         + gates_ref[:, D_MODEL:2 * D_MODEL].astype(F32) * _mm(oc_ref[...], wb_ref[1])
         + gates_ref[:, 2 * D_MODEL:].astype(F32) * _mm(os_ref[...], wb_ref[2]))
    x1 = x_ref[...] + _mm(y.astype(BF16), wo_ref[...])
    x1_ref[...] = x1
    hf = (x1 * _rms(x1)) * g2_ref[...]
    h_hi, h_lo = _split2(hf)
    h2_ref[...] = h_hi
    logits = _mm(h_hi, wr_ref[0]) + (_mm(h_lo, wr_ref[0]) + _mm(h_hi, wr_ref[1])) + br_ref[...]
    comb_ref[...] = _route(logits)


def _merge(og, oc, osw, gates, x, lw, tm):
    T = x.shape[0]
    row = lambda n: pl.BlockSpec((tm, n), lambda i: (i, 0))
    consts = [lw['w_branch'], lw['w_out'], lw['norm_ffn_g'], lw['w_router'], lw['b_router']]
    return pl.pallas_call(
        _merge_body,
        grid=(T // tm,),
        in_specs=[row(BRANCH_W), row(BRANCH_W), row(BRANCH_W), row(3 * D_MODEL), row(D_MODEL)]
                 + [_const_spec(c.shape) for c in consts],
        out_specs=[row(D_MODEL), row(D_MODEL), row(ROUTER_LANES)],
        out_shape=[jax.ShapeDtypeStruct((T, D_MODEL), F32), jax.ShapeDtypeStruct((T, D_MODEL), BF16),
                   jax.ShapeDtypeStruct((T, ROUTER_LANES), F32)],
        compiler_params=_params("parallel"),
        name="merge",
    )(og, oc, osw, gates, x, *consts)


def _moe_body(h_ref, x1_ref, comb_ref, wg_ref, wu_ref, wd_ref, o_ref):
    e = pl.program_id(1)

    @pl.when(e == 0)
    def _():
        o_ref[...] = x1_ref[...]

    h = h_ref[...]
    comb = comb_ref[...]
    lane = lax.broadcasted_iota(jnp.int32, comb.shape, 1)
    c = jnp.sum(jnp.where(lane == e + EXPERT_LANE0, comb, 0.0), -1, keepdims=True)
    a = _mm(h, wg_ref[...])
    u = _mm(h, wu_ref[...])
    hid = (a * jax.nn.sigmoid(a)) * u * c
    o_ref[...] += _mm(hid.astype(BF16), wd_ref[...])


def _moe(h2, x1, comb, lw, tm):
    T = h2.shape[0]
    return pl.pallas_call(
        _moe_body,
        grid=(T // tm, N_EXPERTS),
        in_specs=[pl.BlockSpec((tm, D_MODEL), lambda i, e: (i, 0)),
                  pl.BlockSpec((tm, D_MODEL), lambda i, e: (i, 0)),
                  pl.BlockSpec((tm, ROUTER_LANES), lambda i, e: (i, 0)),
                  pl.BlockSpec((None, D_MODEL, EXPERT_F), lambda i, e: (e, 0, 0)),
                  pl.BlockSpec((None, D_MODEL, EXPERT_F), lambda i, e: (e, 0, 0)),
                  pl.BlockSpec((None, EXPERT_F, D_MODEL), lambda i, e: (e, 0, 0))],
        out_specs=pl.BlockSpec((tm, D_MODEL), lambda i, e: (i, 0)),
        out_shape=jax.ShapeDtypeStruct((T, D_MODEL), F32),
        compiler_params=_params("parallel", "arbitrary"),
        name="moe",
    )(h2, x1, comb, lw['w_exp_gate'], lw['w_exp_up'], lw['w_exp_down'])


def _final_norm_body(x_ref, g_ref, o_ref):
    x = x_ref[...]
    o_ref[...] = (x * _rms(x)) * g_ref[...]


def _final_norm(x, g, tm):
    T = x.shape[0]
    return pl.pallas_call(
        _final_norm_body,
        grid=(T // tm,),
        in_specs=[pl.BlockSpec((tm, D_MODEL), lambda i: (i, 0)), _const_spec(g.shape)],
        out_specs=pl.BlockSpec((tm, D_MODEL), lambda i: (i, 0)),
        out_shape=jax.ShapeDtypeStruct((T, D_MODEL), F32),
        compiler_params=_params("parallel"),
        name="final_norm",
    )(x, g)


def _prep_layer(l, norm_mix_g, w_in, w_fgate2, b_fgate2, gla_norm_g, conv_w, conv_b, conv_ln_g, conv_ln_b,
                swa_sinks, w_branch, w_out, norm_ffn_g, w_router_group, b_router_group, w_router_expert,
                b_router_expert, w_exp_gate, w_exp_up, w_exp_down):
    w = w_in[l]
    c0 = 2 * GLA_KW + 2 * GLA_VW
    c1 = c0 + GATE_RANK
    c2 = c1 + 2 * CONV_C
    c3 = c2 + SWA_HQ * SWA_DH + 2 * SWA_HKV * SWA_DH
    pad_r = LANES - GATE_RANK
    wf2 = jnp.pad(w_fgate2[l], ((0, pad_r), (0, 0)))
    wr = jnp.pad(jnp.concatenate([w_router_group[l], w_router_expert[l]], axis=1),
                 ((0, 0), (0, ROUTER_LANES - N_GROUPS - N_EXPERTS)))
    br = jnp.pad(jnp.concatenate([b_router_group[l], b_router_expert[l]]),
                 (0, ROUTER_LANES - N_GROUPS - N_EXPERTS))
    return {
        'norm_mix_g': norm_mix_g[l][None, :],
        'w_qkvg': w[:, :c0].astype(BF16),
        'w_fr': jnp.pad(w[:, c0:c1], ((0, 0), (0, pad_r))).astype(BF16),
        'w_f2': jnp.stack(_split2(wf2)),
        'b_f2': b_fgate2[l][None, :],
        'w_cglu': w[:, c1:c2].astype(BF16),
        'w_swa': w[:, c2:c3].astype(BF16),
        'w_mg': w[:, c3:].astype(BF16),
        'gla_norm_g': gla_norm_g[l][None, :],
        'conv_w': conv_w[l],
        'conv_b': conv_b[l][None, :],
        'conv_ln_g': conv_ln_g[l][None, :],
        'conv_ln_b': conv_ln_b[l][None, :],
        'swa_sinks': swa_sinks[l],
        'w_branch': w_branch[l].reshape(3, BRANCH_W, D_MODEL).astype(BF16),
        'w_out': w_out[l].astype(BF16),
        'norm_ffn_g': norm_ffn_g[l][None, :],
        'w_router': jnp.stack(_split2(wr)),
        'b_router': br[None, :],
        'w_exp_gate': w_exp_gate[l].astype(BF16),
        'w_exp_up': w_exp_up[l].astype(BF16),
        'w_exp_down': w_exp_down[l].astype(BF16),
    }


def _trunk_layer(x, lw, s0, conv_buf, kprev, vprev, tm):
    B, L, _ = x.shape
    xf = x.reshape(B * L, D_MODEL)
    qkvg, la, u, sq, sk, sv, gates = _in_proj(xf, lw, tm)
    seq = lambda a: a.reshape(B, L, a.shape[-1])
    o_gla, s_new = _gla(seq(qkvg), seq(la), s0, lw['gla_norm_g'], 256)
    o_conv, conv_new = _conv(seq(u), conv_buf, lw['conv_w'], lw['conv_b'], lw['conv_ln_g'], lw['conv_ln_b'], 256)
    o_swa = _swa(seq(sq), seq(sk), seq(sv), kprev, vprev, lw['swa_sinks'])
    flat = lambda a: a.reshape(B * L, a.shape[-1])
    x1, h2, comb = _merge(flat(o_gla), flat(o_conv), flat(o_swa), gates, xf, lw, tm)
    x2 = _moe(h2, x1, comb, lw, tm)
    return x2.reshape(B, L, D_MODEL), s_new, conv_new, seq(sk), seq(sv)


def kernel(x_prompt, x_sample, state_gla, cache_conv, cache_swa_k, cache_swa_v, norm_mix_g, w_in, w_fgate2, b_fgate2, gla_norm_g, conv_w, conv_b, conv_ln_g, conv_ln_b, swa_sinks, w_branch, w_out, norm_ffn_g, w_router_group, b_router_group, w_router_expert, b_router_expert, w_exp_gate, w_exp_up, w_exp_down, final_norm_g):
    depth = w_in.shape[0]
    Bp, Lp, _ = x_prompt.shape
    Bs, Ls, _ = x_sample.shape
    nbuf = cache_swa_k.shape[2]
    kvw = SWA_HKV * SWA_DH
    xp, xs = x_prompt, x_sample
    gla_p, gla_s, conv_p, conv_s, kp, ks_, vp, vs = [], [], [], [], [], [], [], []
    keep = min(WINDOW, Lp)
    for l in range(depth):
        lw = _prep_layer(l, norm_mix_g, w_in, w_fgate2, b_fgate2, gla_norm_g, conv_w, conv_b, conv_ln_g,
                         conv_ln_b, swa_sinks, w_branch, w_out, norm_ffn_g, w_router_group, b_router_group,
                         w_router_expert, b_router_expert, w_exp_gate, w_exp_up, w_exp_down)
        s0 = jnp.zeros((Bp, GLA_HEADS, GLA_DK, GLA_DV), F32)
        cb0 = jnp.zeros((Bp, CONV_K - 1, CONV_C), F32)
        xp, a1, a2, k_all, v_all = _trunk_layer(xp, lw, s0, cb0, None, None, 512)
        gla_p.append(a1)
        conv_p.append(a2)
        kp.append(k_all[:, -keep:].reshape(Bp, keep, SWA_HKV, SWA_DH))
        vp.append(v_all[:, -keep:].reshape(Bp, keep, SWA_HKV, SWA_DH))
        ck = cache_swa_k[l].reshape(Bs, nbuf, kvw)
        cv = cache_swa_v[l].reshape(Bs, nbuf, kvw)
        xs, b1, b2, k_new, v_new = _trunk_layer(xs, lw, state_gla[l], cache_conv[l], ck, cv, 512)
        gla_s.append(b1)
        conv_s.append(b2)
        ks_.append(jnp.concatenate([ck, k_new], axis=1)[:, -nbuf:].reshape(Bs, nbuf, SWA_HKV, SWA_DH))
        vs.append(jnp.concatenate([cv, v_new], axis=1)[:, -nbuf:].reshape(Bs, nbuf, SWA_HKV, SWA_DH))
    g = final_norm_g[None, :]
    y_prompt = _final_norm(xp.reshape(Bp * Lp, D_MODEL), g, 512).reshape(Bp, Lp, D_MODEL)
    y_sample = _final_norm(xs.reshape(Bs * Ls, D_MODEL), g, 512).reshape(Bs, Ls, D_MODEL)
    return (y_prompt, y_sample, jnp.stack(gla_p), jnp.stack(gla_s), jnp.stack(conv_p), jnp.stack(conv_s),
            jnp.stack(kp), jnp.stack(ks_), jnp.stack(vp), jnp.stack(vs))
```

```python
import functools

import numpy as np
import jax
import jax.numpy as jnp
from jax import lax
from jax.experimental import pallas as pl
from jax.experimental.pallas import tpu as pltpu

F32 = jnp.float32
BF16 = jnp.bfloat16

D_MODEL = 1024
BRANCH_W = 512
GLA_HEADS = 4
GLA_DK = 64
GLA_DV = 128
GLA_KW = GLA_HEADS * GLA_DK
GLA_VW = GLA_HEADS * GLA_DV
GATE_RANK = 16
GATE_NORMALIZER = 16.0
GLA_CHUNK = 64
CONV_C = 512
CONV_K = 31
SWA_HQ = 8
SWA_HKV = 2
SWA_G = SWA_HQ // SWA_HKV
SWA_DH = 64
WINDOW = 128
N_GROUPS = 4
EXPERTS_PER_GROUP = 8
N_EXPERTS = 32
EXPERT_F = 256
EPS = 1e-6
PAST_LEN = 16384

LANES = 128
VMEM_LIMIT_BYTES = 56 * 2**20
ROUTER_LANES = LANES
EXPERT_LANE0 = N_GROUPS
CONV_HIST = 32
MOE_TM = 256
TOP_K = 2
TOKEN_TILE = 512
SEQ_TILE = 256


def _mm(a, b):
    return jnp.dot(a, b, preferred_element_type=F32)


def _mm_nt(a, b):
    return lax.dot_general(a, b, (((1,), (1,)), ((), ())), preferred_element_type=F32)


def _mm_tn(a, b):
    return lax.dot_general(a, b, (((0,), (0,)), ((), ())), preferred_element_type=F32)


def _split2(x):
    hi = x.astype(BF16)
    lo = (x - hi.astype(F32)).astype(BF16)
    return hi, lo


def _split3(x):
    hi = x.astype(BF16)
    r = x - hi.astype(F32)
    mid = r.astype(BF16)
    lo = (r - mid.astype(F32)).astype(BF16)
    return hi, mid, lo


def _const_spec(shape):
    nd = len(shape)
    return pl.BlockSpec(shape, lambda *_: (0,) * nd)


def _params(*sem):
    return pltpu.CompilerParams(dimension_semantics=sem, vmem_limit_bytes=VMEM_LIMIT_BYTES)


def _rms(x):
    return lax.rsqrt(jnp.mean(x * x, axis=-1, keepdims=True) + EPS)


def _log_sigmoid(z):
    return jnp.minimum(z, 0.0) - jnp.log1p(jnp.exp(-jnp.abs(z)))


def _in_proj_body(x_ref, g_ref, wqkvg_ref, wfr_ref, wf2_ref, bf2_ref, wcglu_ref, wswa_ref, wmg_ref,
                  qkvg_ref, la_ref, u_ref, sq_ref, sk_ref, sv_ref, gates_ref):
    x = x_ref[...]
    h = ((x * _rms(x)) * g_ref[...]).astype(BF16)
    qkvg_ref[...] = _mm(h, wqkvg_ref[...])
    gfr_hi, gfr_lo = _split2(_mm(h, wfr_ref[...]))
    z = _mm(gfr_hi, wf2_ref[0]) + (_mm(gfr_lo, wf2_ref[0]) + _mm(gfr_hi, wf2_ref[1])) + bf2_ref[...]
    la_ref[...] = _log_sigmoid(z) * (1.0 / GATE_NORMALIZER)
    cglu = _mm(h, wcglu_ref[...])
    u_ref[...] = cglu[:, :CONV_C] * jax.nn.sigmoid(cglu[:, CONV_C:])
    swa = _mm(h, wswa_ref[...])
    sq_ref[...] = swa[:, :SWA_HQ * SWA_DH] * (SWA_DH ** -0.5)
    sk_ref[...] = swa[:, SWA_HQ * SWA_DH:SWA_HQ * SWA_DH + SWA_HKV * SWA_DH]
    sv_ref[...] = swa[:, SWA_HQ * SWA_DH + SWA_HKV * SWA_DH:]
    gates_ref[...] = jax.nn.sigmoid(_mm(h, wmg_ref[...])).astype(BF16)


def _in_proj(x, lw, tm):
    T = x.shape[0]
    row = lambda n: pl.BlockSpec((tm, n), lambda i: (i, 0))
    outs = [(GLA_KW * 2 + GLA_VW * 2, F32), (GLA_KW, F32), (CONV_C, F32), (SWA_HQ * SWA_DH, F32),
            (SWA_HKV * SWA_DH, F32), (SWA_HKV * SWA_DH, F32), (3 * D_MODEL, BF16)]
    consts = [lw['norm_mix_g'], lw['w_qkvg'], lw['w_fr'], lw['w_f2'], lw['b_f2'], lw['w_cglu'], lw['w_swa'],
              lw['w_mg']]
    return pl.pallas_call(
        _in_proj_body,
        grid=(T // tm,),
        in_specs=[row(D_MODEL)] + [_const_spec(c.shape) for c in consts],
        out_specs=[row(n) for n, _ in outs],
        out_shape=[jax.ShapeDtypeStruct((T, n), dt) for n, dt in outs],
        compiler_params=_params("parallel"),
        name="in_proj",
    )(x, *consts)


def _gla_levels(C):
    return [w for w in (32, 16, 8, 4, 2, 1) if w < C]


def _gla_sum_matrix(C):
    t = np.arange(C)
    mats = [(t[:, None] >= t[None, :]).astype(np.float32)]
    qs, ks = [], []
    for w in _gla_levels(C):
        start = (t // (2 * w)) * (2 * w)
        upper = (t % (2 * w)) >= w
        mq = upper[:, None] & (t[None, :] >= (start + w)[:, None]) & (t[None, :] <= t[:, None])
        mk = (~upper)[:, None] & (t[None, :] > t[:, None]) & (t[None, :] <= (start + w - 1)[:, None])
        qs.append(mq.astype(np.float32))
        ks.append(mk.astype(np.float32))
    return np.concatenate(mats + qs + ks, axis=0)


def _gla_body(qkvg_ref, la_ref, s0_ref, m_ref, gng_ref, o_ref, s_out_ref, st_ref, *, C, TT, exact):
    levels = _gla_levels(C)
    n = len(levels)
    mmdt = F32 if exact else BF16
    step = pl.program_id(1)

    @pl.when(step == 0)
    def _():
        for h in range(GLA_HEADS):
            st_ref[h] = s0_ref[h].T

    def chunk(c, carry):
        rows = pl.ds(pl.multiple_of(c * C, C), C)
        qkvg = qkvg_ref[rows, :]
        la = la_ref[rows, :]
        if exact:
            e_all = jnp.dot(m_ref[...], la, precision=lax.Precision.HIGHEST, preferred_element_type=F32)
        else:
            e_all = _mm(m_ref[...], jnp.concatenate(_split3(la), axis=0))
        b = e_all[0:C]
        q = qkvg[:, 0:GLA_KW] * (GLA_DK ** -0.5)
        k = qkvg[:, GLA_KW:2 * GLA_KW]
        v = qkvg[:, 2 * GLA_KW:2 * GLA_KW + GLA_VW]
        g = qkvg[:, 2 * GLA_KW + GLA_VW:]
        rowi = lax.broadcasted_iota(jnp.int32, (C, GLA_KW), 0)
        qs, ks = [], []
        for li, w in enumerate(levels):
            eq = e_all[(1 + li) * C:(2 + li) * C]
            ek = e_all[(1 + n + li) * C:(2 + n + li) * C]
            upper = (rowi & w) != 0
            qs.append(jnp.where(upper, q * jnp.exp(eq), 0.0).astype(mmdt))
            ks.append(jnp.where(upper, 0.0, k * jnp.exp(ek)).astype(mmdt))
        qb = (q * jnp.exp(b)).astype(mmdt)
        b_last = b[C - 1:C, :]
        kd = (k * jnp.exp(b_last - b)).astype(mmdt)
        dec = jnp.exp(b_last)
        qm = q.astype(mmdt)
        km = k.astype(mmdt)
        ti = lax.broadcasted_iota(jnp.int32, (C, C), 0)
        si = lax.broadcasted_iota(jnp.int32, (C, C), 1)
        for h in range(GLA_HEADS):
            hl = slice(h * GLA_DK, (h + 1) * GLA_DK)
            vl = slice(h * GLA_DV, (h + 1) * GLA_DV)
            attn = jnp.where(ti == si, _mm_nt(qm[:, hl], km[:, hl]), 0.0)
            for li, w in enumerate(levels):
                p = _mm_nt(qs[li][:, hl], ks[li][:, hl])
                if 2 * w < C:
                    sh = (2 * w).bit_length() - 1
                    p = jnp.where(lax.shift_right_logical(ti, sh) == lax.shift_right_logical(si, sh), p, 0.0)
                attn = attn + p
            vh = v[:, vl].astype(mmdt)
            st = st_ref[h]
            o = _mm(attn.astype(mmdt), vh) + _mm_nt(qb[:, hl], st.astype(mmdt))
            st_ref[h] = st * dec[:, hl] + _mm_tn(vh, kd[:, hl])
            gh = g[:, vl]
            o = ((o * _rms(o)) * gng_ref[...]) * (gh * jax.nn.sigmoid(gh))
            o_ref[rows, vl] = o.astype(o_ref.dtype)
        return carry

    if TT == C:
        chunk(0, 0)
    else:
        lax.fori_loop(0, TT // C, chunk, 0)

    @pl.when(step == pl.num_programs(1) - 1)
    def _():
        for h in range(GLA_HEADS):
            s_out_ref[h] = st_ref[h].T


def _gla(qkvg, la, s0, gng, B, L, row0, TT):
    C = min(GLA_CHUNK, L)
    TT = min(TT, L)
    nT = L // TT
    off = row0 // TT
    exact = C < 16
    m = _gla_sum_matrix(C)
    m = jnp.asarray(m, F32) if exact else jnp.asarray(np.concatenate([m, m, m], axis=1), BF16)
    body = functools.partial(_gla_body, C=C, TT=TT, exact=exact)
    return pl.pallas_call(
        body,
        grid=(B, nT),
        in_specs=[pl.BlockSpec((TT, qkvg.shape[1]), lambda b, i: (off + b * nT + i, 0)),
                  pl.BlockSpec((TT, GLA_KW), lambda b, i: (off + b * nT + i, 0)),
                  pl.BlockSpec((None, GLA_HEADS, GLA_DK, GLA_DV), lambda b, i: (b, 0, 0, 0)),
                  _const_spec(m.shape), _const_spec(gng.shape)],
        out_specs=[pl.BlockSpec((TT, GLA_VW), lambda b, i: (b * nT + i, 0)),
                   pl.BlockSpec((None, GLA_HEADS, GLA_DK, GLA_DV), lambda b, i: (b, 0, 0, 0))],
        out_shape=[jax.ShapeDtypeStruct((B * L, GLA_VW), BF16),
                   jax.ShapeDtypeStruct((B, GLA_HEADS, GLA_DK, GLA_DV), F32)],
        scratch_shapes=[pltpu.VMEM((GLA_HEADS, GLA_DV, GLA_DK), F32)],
        compiler_params=_params("parallel", "arbitrary"),
        name="gla",
    )(qkvg, la, s0, m, gng)


def _conv_body(u_ref, buf_ref, w_ref, cb_ref, lg_ref, lb_ref, o_ref, new_ref, ext_ref, *, TT):
    lo = CONV_HIST - (CONV_K - 1)

    @pl.when(pl.program_id(1) == 0)
    def _():
        ext_ref[lo:CONV_HIST, :] = buf_ref[...]

    ext_ref[CONV_HIST:CONV_HIST + TT, :] = u_ref[...]
    acc = jnp.zeros((TT, CONV_C), F32) + cb_ref[...]
    for j in range(CONV_K):
        acc = acc + w_ref[j:j + 1, :] * ext_ref[lo + j:lo + j + TT, :]
    mu = jnp.mean(acc, axis=-1, keepdims=True)
    d = acc - mu
    var = jnp.mean(d * d, axis=-1, keepdims=True)
    y = (d * lax.rsqrt(var + EPS)) * lg_ref[...] + lb_ref[...]
    o_ref[...] = (y * jax.nn.sigmoid(y)).astype(o_ref.dtype)
    tail = ext_ref[TT + lo:TT + CONV_HIST, :]
    new_ref[...] = tail
    ext_ref[lo:CONV_HIST, :] = tail


def _conv(u, buf, w, cb, lg, lb, B, L, row0, TT):
    TT = min(TT, L)
    nT = L // TT
    off = row0 // TT
    body = functools.partial(_conv_body, TT=TT)
    return pl.pallas_call(
        body,
        grid=(B, nT),
        in_specs=[pl.BlockSpec((TT, CONV_C), lambda b, i: (off + b * nT + i, 0)),
                  pl.BlockSpec((None, CONV_K - 1, CONV_C), lambda b, i: (b, 0, 0)),
                  _const_spec(w.shape), _const_spec(cb.shape), _const_spec(lg.shape), _const_spec(lb.shape)],
        out_specs=[pl.BlockSpec((TT, CONV_C), lambda b, i: (b * nT + i, 0)),
                   pl.BlockSpec((None, CONV_K - 1, CONV_C), lambda b, i: (b, 0, 0))],
        out_shape=[jax.ShapeDtypeStruct((B * L, CONV_C), BF16),
                   jax.ShapeDtypeStruct((B, CONV_K - 1, CONV_C), F32)],
        scratch_shapes=[pltpu.VMEM((CONV_HIST + TT, CONV_C), F32)],
        compiler_params=_params("parallel", "arbitrary"),
        name="conv",
    )(u, buf, w, cb, lg, lb)


def _swa_body(q_ref, kp_ref, kc_ref, vp_ref, vc_ref, sink_ref, o_ref, *, TQ, first_has_history, mmdt):
    q = q_ref[...].astype(mmdt)
    kp = kp_ref[...].astype(mmdt)
    kc = kc_ref[...].astype(mmdt)
    vp = vp_ref[...].astype(mmdt)
    vc = vc_ref[...].astype(mmdt)
    i1 = lax.broadcasted_iota(jnp.int32, (TQ, WINDOW), 0)
    j1 = lax.broadcasted_iota(jnp.int32, (TQ, WINDOW), 1)
    ok1 = j1 >= i1
    if not first_has_history:
        ok1 = jnp.logical_and(ok1, pl.program_id(1) > 0)
    d1 = (WINDOW + i1 - j1).astype(F32)
    i2 = lax.broadcasted_iota(jnp.int32, (TQ, TQ), 0)
    j2 = lax.broadcasted_iota(jnp.int32, (TQ, TQ), 1)
    ok2 = j2 <= i2
    d2 = (i2 - j2).astype(F32)
    for h in range(SWA_HQ):
        ql = slice(h * SWA_DH, (h + 1) * SWA_DH)
        kl = slice((h // SWA_G) * SWA_DH, (h // SWA_G + 1) * SWA_DH)
        slope = 2.0 ** (-8.0 * (h + 1) / SWA_HQ)
        s1 = jnp.where(ok1, _mm_nt(q[:, ql], kp[:, kl]) - slope * d1, -jnp.inf)
        s2 = jnp.where(ok2, _mm_nt(q[:, ql], kc[:, kl]) - slope * d2, -jnp.inf)
        sink = sink_ref[h]
        m = jnp.maximum(jnp.maximum(jnp.max(s1, -1, keepdims=True), jnp.max(s2, -1, keepdims=True)), sink)
        p1 = jnp.exp(s1 - m)
        p2 = jnp.exp(s2 - m)
        den = jnp.sum(p1, -1, keepdims=True) + jnp.sum(p2, -1, keepdims=True) + jnp.exp(sink - m)
        o = (_mm(p1.astype(mmdt), vp[:, kl]) + _mm(p2.astype(mmdt), vc[:, kl])) / den
        o_ref[:, ql] = o.astype(o_ref.dtype)


def _swa(q, k, v, kprev, vprev, sinks, B, L, row0):
    TQ = min(WINDOW, L)
    nT = L // TQ
    off = row0 // TQ
    kvw = SWA_HKV * SWA_DH
    cur = pl.BlockSpec((TQ, kvw), lambda b, n: (off + b * nT + n, 0))
    from_start = kprev is None
    if from_start:
        assert TQ == WINDOW
        prev = pl.BlockSpec((WINDOW, kvw), lambda b, n: (off + b * nT + jnp.maximum(n - 1, 0), 0))
        kprev, vprev = k, v
    else:
        assert L == TQ
        prev = pl.BlockSpec((None, WINDOW, kvw), lambda b, n: (b, 0, 0))
    body = functools.partial(_swa_body, TQ=TQ, first_has_history=not from_start, mmdt=BF16 if TQ >= 16 else F32)
    return pl.pallas_call(
        body,
        grid=(B, nT),
        in_specs=[pl.BlockSpec((TQ, SWA_HQ * SWA_DH), lambda b, n: (off + b * nT + n, 0)),
                  prev, cur, prev, cur, pl.BlockSpec(memory_space=pltpu.SMEM)],
        out_specs=pl.BlockSpec((TQ, SWA_HQ * SWA_DH), lambda b, n: (b * nT + n, 0)),
        out_shape=jax.ShapeDtypeStruct((B * L, SWA_HQ * SWA_DH), BF16),
        compiler_params=_params("parallel", "arbitrary"),
        name="swa",
    )(q, kprev, k, vprev, v, sinks)


def _route(logits):
    lane = lax.broadcasted_iota(jnp.int32, logits.shape, 1)
    big = jnp.int32(2 * ROUTER_LANES)
    is_g = lane < N_GROUPS
    gl = jnp.where(is_g, logits, -jnp.inf)
    gmax = jnp.max(gl, -1, keepdims=True)
    gsel = jnp.min(jnp.where(gl == gmax, lane, big), -1, keepdims=True)
    gprob = 1.0 / jnp.sum(jnp.where(is_g, jnp.exp(logits - gmax), 0.0), -1, keepdims=True)
    e = lane - EXPERT_LANE0
    in_grp = jnp.logical_and(jnp.logical_and(e >= 0, e < N_EXPERTS),
                             lax.shift_right_logical(jnp.maximum(e, 0), 3) == gsel)
    el = jnp.where(in_grp, logits, -jnp.inf)
    v1 = jnp.max(el, -1, keepdims=True)
    i1 = jnp.min(jnp.where(el == v1, lane, big), -1, keepdims=True)
    el2 = jnp.where(lane == i1, -jnp.inf, el)
    v2 = jnp.max(el2, -1, keepdims=True)
    i2 = jnp.min(jnp.where(el2 == v2, lane, big), -1, keepdims=True)
    t = jnp.exp(v2 - v1)
    w1 = gprob / (1.0 + t)
    w2 = gprob * t / (1.0 + t)
    return lane, i1, i2, w1, w2


def _merge_body(og_ref, oc_ref, os_ref, gates_ref, x_ref, wb_ref, wo_ref, g2_ref, wr_ref, br_ref, tri_ref,
                x1_ref, hf_ref, ridx_ref, rw_ref, cnt_ref, seen_ref):
    @pl.when(pl.program_id(0) == 0)
    def _():
        seen_ref[...] = jnp.zeros_like(seen_ref)

    y = (gates_ref[:, 0:D_MODEL].astype(F32) * _mm(og_ref[...], wb_ref[0])
         + gates_ref[:, D_MODEL:2 * D_MODEL].astype(F32) * _mm(oc_ref[...], wb_ref[1])
         + gates_ref[:, 2 * D_MODEL:].astype(F32) * _mm(os_ref[...], wb_ref[2]))
    x1 = x_ref[...] + _mm(y.astype(BF16), wo_ref[...])
    x1_ref[...] = x1
    hf = (x1 * _rms(x1)) * g2_ref[...]
    hf_ref[...] = hf
    h_hi, h_lo = _split2(hf)
    logits = _mm(h_hi, wr_ref[0]) + (_mm(h_lo, wr_ref[0]) + _mm(h_hi, wr_ref[1])) + br_ref[...]
    lane, i1, i2, w1, w2 = _route(logits)
    onehot = jnp.where(jnp.logical_or(lane == i1, lane == i2), 1.0, 0.0)
    before = _mm(tri_ref[...], onehot.astype(BF16)) + seen_ref[...]
    r1 = jnp.sum(jnp.where(lane == i1, before, 0.0), -1, keepdims=True).astype(jnp.int32)
    r2 = jnp.sum(jnp.where(lane == i2, before, 0.0), -1, keepdims=True).astype(jnp.int32)
    seen = seen_ref[...] + jnp.sum(onehot, axis=0, keepdims=True)
    seen_ref[...] = seen
    cnt_ref[...] = jnp.broadcast_to(seen, cnt_ref.shape)
    ridx_ref[...] = jnp.where(lane == 0, i1 - EXPERT_LANE0,
                              jnp.where(lane == 1, i2 - EXPERT_LANE0,
                                        jnp.where(lane == 2, r1, jnp.where(lane == 3, r2, 0))))
    rw_ref[...] = jnp.where(lane == 0, w1, jnp.where(lane == 1, w2, 0.0))


def _merge(og, oc, osw, gates, x, lw, tm):
    T = x.shape[0]
    row = lambda n: pl.BlockSpec((tm, n), lambda i: (i, 0))
    tri = jnp.asarray(np.tril(np.ones((tm, tm), np.float32), -1), BF16)
    consts = [lw['w_branch'], lw['w_out'], lw['norm_ffn_g'], lw['w_router'], lw['b_router'], tri]
    return pl.pallas_call(
        _merge_body,
        grid=(T // tm,),
        in_specs=[row(BRANCH_W), row(BRANCH_W), row(BRANCH_W), row(3 * D_MODEL), row(D_MODEL)]
                 + [_const_spec(c.shape) for c in consts],
        out_specs=[row(D_MODEL), row(D_MODEL), row(ROUTER_LANES), row(ROUTER_LANES),
                   _const_spec((8, ROUTER_LANES))],
        out_shape=[jax.ShapeDtypeStruct((T, D_MODEL), F32), jax.ShapeDtypeStruct((T, D_MODEL), F32),
                   jax.ShapeDtypeStruct((T, ROUTER_LANES), jnp.int32),
                   jax.ShapeDtypeStruct((T, ROUTER_LANES), F32),
                   jax.ShapeDtypeStruct((8, ROUTER_LANES), F32)],
        scratch_shapes=[pltpu.VMEM((1, ROUTER_LANES), F32)],
        compiler_params=_params("arbitrary"),
        name="merge",
    )(og, oc, osw, gates, x, *consts)


def _moe_tiles(T):
    return -(-TOP_K * T // MOE_TM) + N_EXPERTS


def _dispatch_plan(ridx, cnt, T):
    NT = _moe_tiles(T)
    n_rows = NT * MOE_TM
    counts = cnt[0, EXPERT_LANE0:EXPERT_LANE0 + N_EXPERTS].astype(jnp.int32)
    padded = ((counts + MOE_TM - 1) // MOE_TM) * MOE_TM
    ends = jnp.cumsum(padded)
    start = ends - padded
    tok = jnp.arange(T, dtype=jnp.int32)
    d1 = start[ridx[:, 0]] + ridx[:, 2]
    d2 = start[ridx[:, 1]] + ridx[:, 3]
    row = jnp.arange(n_rows, dtype=jnp.int32)
    spare = TOP_K * T + ((row // MOE_TM) % 2) * MOE_TM + row % MOE_TM
    src = jnp.zeros((n_rows,), jnp.int32).at[d1].set(tok, unique_indices=True).at[d2].set(tok, unique_indices=True)
    dst = spare.at[d1].set(tok, unique_indices=True).at[d2].set(T + tok, unique_indices=True)
    tile_end = ends // MOE_TM
    n_used = tile_end[-1:]
    tile_expert = jnp.searchsorted(tile_end, jnp.minimum(jnp.arange(NT, dtype=jnp.int32), n_used - 1),
                                   side='right').astype(jnp.int32)
    return src.reshape(NT, 1, MOE_TM), dst.reshape(NT, 1, MOE_TM), tile_expert, n_used.astype(jnp.int32)


def _ffn_body(te_ref, nu_ref, src_next_ref, src_first_ref, dst_ref, h_hbm, wg_ref, wu_ref, wd_ref, y_hbm,
              xbuf, ybuf, gsem, ssem):
    j = pl.program_id(0)
    last = pl.num_programs(0) - 1
    nu = nu_ref[0]
    slot = lax.rem(j, 2)

    def start_gather(idx_ref, s):
        def body(r, c):
            pltpu.make_async_copy(h_hbm.at[pl.ds(idx_ref[0, 0, r], 1)], xbuf.at[s, pl.ds(r, 1)], gsem.at[s]).start()
            return c
        lax.fori_loop(0, MOE_TM, body, 0, unroll=8)

    def wait_gather(s):
        pltpu.make_async_copy(h_hbm.at[pl.ds(0, MOE_TM)], xbuf.at[s], gsem.at[s]).wait()

    def wait_scatter(s):
        pltpu.make_async_copy(ybuf.at[s], y_hbm.at[pl.ds(0, MOE_TM)], ssem.at[s]).wait()

    @pl.when(j == 0)
    def _():
        start_gather(src_first_ref, 0)
        ybuf[...] = jnp.zeros_like(ybuf)
        spare0 = y_hbm.shape[0] - 2 * MOE_TM
        for s in range(2):
            init = pltpu.make_async_copy(ybuf.at[s], y_hbm.at[pl.ds(spare0 + s * MOE_TM, MOE_TM)], ssem.at[s])
            init.start()
            init.wait()

    @pl.when(jnp.logical_and(j >= 2, j - 2 < nu))
    def _():
        wait_scatter(slot)

    @pl.when(j < nu)
    def _():
        wait_gather(slot)

        @pl.when(j + 1 < nu)
        def _():
            start_gather(src_next_ref, 1 - slot)

        x = xbuf[slot].astype(BF16)
        a = _mm(x, wg_ref[...])
        u = _mm(x, wu_ref[...])
        hid = (a * jax.nn.sigmoid(a)) * u
        ybuf[slot] = _mm(hid.astype(BF16), wd_ref[...])

        def body(r, c):
            pltpu.make_async_copy(ybuf.at[slot, pl.ds(r, 1)], y_hbm.at[pl.ds(dst_ref[0, 0, r], 1)],
                                  ssem.at[slot]).start()
            return c
        lax.fori_loop(0, MOE_TM, body, 0, unroll=8)

    @pl.when(j == last)
    def _():
        @pl.when(jnp.logical_and(j >= 1, j - 1 < nu))
        def _():
            wait_scatter(1 - slot)

        @pl.when(j < nu)
        def _():
            wait_scatter(slot)


def _ffn(hf, plan, lw):
    T = hf.shape[0]
    src, dst, tile_expert, n_used = plan
    NT = src.shape[0]
    idx = lambda f: pl.BlockSpec((1, 1, MOE_TM), f, memory_space=pltpu.SMEM)
    wspec = lambda shape: pl.BlockSpec((None,) + shape, lambda j, te, nu: (te[j], 0, 0))
    grid_spec = pltpu.PrefetchScalarGridSpec(
        num_scalar_prefetch=2,
        grid=(NT,),
        in_specs=[idx(lambda j, te, nu: (jnp.minimum(j + 1, NT - 1), 0, 0)),
                  idx(lambda j, te, nu: (0, 0, 0)),
                  idx(lambda j, te, nu: (j, 0, 0)),
                  pl.BlockSpec(memory_space=pl.ANY),
                  wspec((D_MODEL, EXPERT_F)), wspec((D_MODEL, EXPERT_F)), wspec((EXPERT_F, D_MODEL))],
        out_specs=pl.BlockSpec(memory_space=pl.ANY),
        scratch_shapes=[pltpu.VMEM((2, MOE_TM, D_MODEL), F32), pltpu.VMEM((2, MOE_TM, D_MODEL), F32),
                        pltpu.SemaphoreType.DMA((2,)), pltpu.SemaphoreType.DMA((2,))])
    return pl.pallas_call(
        _ffn_body,
        grid_spec=grid_spec,
        out_shape=jax.ShapeDtypeStruct((TOP_K * T + 2 * MOE_TM, D_MODEL), F32),
        compiler_params=_params("arbitrary"),
        name="ffn",
    )(tile_expert, n_used, src, src, dst, hf, lw['w_exp_gate'], lw['w_exp_up'], lw['w_exp_down'])


def _combine_body(x1_ref, ya_ref, yb_ref, rw_ref, g_ref, o_ref, *, final):
    rw = rw_ref[...]
    lane = lax.broadcasted_iota(jnp.int32, rw.shape, 1)
    w1 = jnp.sum(jnp.where(lane == 0, rw, 0.0), -1, keepdims=True)
    w2 = jnp.sum(jnp.where(lane == 1, rw, 0.0), -1, keepdims=True)
    x2 = x1_ref[...] + (w1 * ya_ref[...] + w2 * yb_ref[...])
    if final:
        x2 = (x2 * _rms(x2)) * g_ref[...]
    o_ref[...] = x2


def _combine(x1, y, rw, g, row0, rows, tm, final):
    T = x1.shape[0]
    off = row0 // tm
    body = functools.partial(_combine_body, final=final)
    return pl.pallas_call(
        body,
        grid=(rows // tm,),
        in_specs=[pl.BlockSpec((tm, D_MODEL), lambda i: (off + i, 0)),
                  pl.BlockSpec((tm, D_MODEL), lambda i: (off + i, 0)),
                  pl.BlockSpec((tm, D_MODEL), lambda i: (T // tm + off + i, 0)),
                  pl.BlockSpec((tm, ROUTER_LANES), lambda i: (off + i, 0)),
                  _const_spec(g.shape)],
        out_specs=pl.BlockSpec((tm, D_MODEL), lambda i: (i, 0)),
        out_shape=jax.ShapeDtypeStruct((rows, D_MODEL), F32),
        compiler_params=_params("parallel"),
        name="combine",
    )(x1, y, y, rw, g)


def _prep_layer(l, norm_mix_g, w_in, w_fgate2, b_fgate2, gla_norm_g, conv_w, conv_b, conv_ln_g, conv_ln_b,
                swa_sinks, w_branch, w_out, norm_ffn_g, w_router_group, b_router_group, w_router_expert,
                b_router_expert, w_exp_gate, w_exp_up, w_exp_down):
    w = w_in[l]
    c0 = 2 * GLA_KW + 2 * GLA_VW
    c1 = c0 + GATE_RANK
    c2 = c1 + 2 * CONV_C
    c3 = c2 + SWA_HQ * SWA_DH + 2 * SWA_HKV * SWA_DH
    pad_r = LANES - GATE_RANK
    wf2 = jnp.pad(w_fgate2[l], ((0, pad_r), (0, 0)))
    wr = jnp.pad(jnp.concatenate([w_router_group[l], w_router_expert[l]], axis=1),
                 ((0, 0), (0, ROUTER_LANES - N_GROUPS - N_EXPERTS)))
    br = jnp.pad(jnp.concatenate([b_router_group[l], b_router_expert[l]]),
                 (0, ROUTER_LANES - N_GROUPS - N_EXPERTS))
    return {
        'norm_mix_g': norm_mix_g[l][None, :],
        'w_qkvg': w[:, :c0].astype(BF16),
        'w_fr': jnp.pad(w[:, c0:c1], ((0, 0), (0, pad_r))).astype(BF16),
        'w_f2': jnp.stack(_split2(wf2)),
        'b_f2': b_fgate2[l][None, :],
        'w_cglu': w[:, c1:c2].astype(BF16),
        'w_swa': w[:, c2:c3].astype(BF16),
        'w_mg': w[:, c3:].astype(BF16),
        'gla_norm_g': gla_norm_g[l][None, :],
        'conv_w': conv_w[l],
        'conv_b': conv_b[l][None, :],
        'conv_ln_g': conv_ln_g[l][None, :],
        'conv_ln_b': conv_ln_b[l][None, :],
        'swa_sinks': swa_sinks[l],
        'w_branch': w_branch[l].reshape(3, BRANCH_W, D_MODEL).astype(BF16),
        'w_out': w_out[l].astype(BF16),
        'norm_ffn_g': norm_ffn_g[l][None, :],
        'w_router': jnp.stack(_split2(wr)),
        'b_router': br[None, :],
        'w_exp_gate': w_exp_gate[l].astype(BF16),
        'w_exp_up': w_exp_up[l].astype(BF16),
        'w_exp_down': w_exp_down[l].astype(BF16),
    }


def kernel(x_prompt, x_sample, state_gla, cache_conv, cache_swa_k, cache_swa_v, norm_mix_g, w_in, w_fgate2, b_fgate2, gla_norm_g, conv_w, conv_b, conv_ln_g, conv_ln_b, swa_sinks, w_branch, w_out, norm_ffn_g, w_router_group, b_router_group, w_router_expert, b_router_expert, w_exp_gate, w_exp_up, w_exp_down, final_norm_g):
    depth = w_in.shape[0]
    Bp, Lp, _ = x_prompt.shape
    Bs, Ls, _ = x_sample.shape
    Tp, Ts = Bp * Lp, Bs * Ls
    T = Tp + Ts
    nbuf = cache_swa_k.shape[2]
    assert nbuf == WINDOW and PAST_LEN >= nbuf
    kvw = SWA_HKV * SWA_DH
    x = jnp.concatenate([x_prompt.reshape(Tp, D_MODEL), x_sample.reshape(Ts, D_MODEL)], axis=0)
    gla_p, gla_s, conv_p, conv_s, kp, ks_, vp, vs = [], [], [], [], [], [], [], []
    keep = min(WINDOW, Lp)
    g_final = final_norm_g[None, :]
    for l in range(depth):
        lw = _prep_layer(l, norm_mix_g, w_in, w_fgate2, b_fgate2, gla_norm_g, conv_w, conv_b, conv_ln_g,
                         conv_ln_b, swa_sinks, w_branch, w_out, norm_ffn_g, w_router_group, b_router_group,
                         w_router_expert, b_router_expert, w_exp_gate, w_exp_up, w_exp_down)
        qkvg, la, u, sq, sk, sv, gates = _in_proj(x, lw, TOKEN_TILE)
        s0 = jnp.zeros((Bp, GLA_HEADS, GLA_DK, GLA_DV), F32)
        cb0 = jnp.zeros((Bp, CONV_K - 1, CONV_C), F32)
        og_p, a1 = _gla(qkvg, la, s0, lw['gla_norm_g'], Bp, Lp, 0, SEQ_TILE)
        oc_p, a2 = _conv(u, cb0, lw['conv_w'], lw['conv_b'], lw['conv_ln_g'], lw['conv_ln_b'], Bp, Lp, 0, SEQ_TILE)
        os_p = _swa(sq, sk, sv, None, None, lw['swa_sinks'], Bp, Lp, 0)
        ck = cache_swa_k[l].reshape(Bs, nbuf, kvw)
        cv = cache_swa_v[l].reshape(Bs, nbuf, kvw)
        og_s, b1 = _gla(qkvg, la, state_gla[l], lw['gla_norm_g'], Bs, Ls, Tp, SEQ_TILE)
        oc_s, b2 = _conv(u, cache_conv[l], lw['conv_w'], lw['conv_b'], lw['conv_ln_g'], lw['conv_ln_b'],
                         Bs, Ls, Tp, SEQ_TILE)
        os_s = _swa(sq, sk, sv, ck, cv, lw['swa_sinks'], Bs, Ls, Tp)
        cat = lambda a, b: jnp.concatenate([a, b], axis=0)
        x1, hf, ridx, rw, cnt = _merge(cat(og_p, og_s), cat(oc_p, oc_s), cat(os_p, os_s), gates, x, lw, TOKEN_TILE)
        y = _ffn(hf, _dispatch_plan(ridx, cnt, T), lw)
        if l + 1 < depth:
            x = _combine(x1, y, rw, g_final, 0, T, TOKEN_TILE, False)
        else:
            y_prompt = _combine(x1, y, rw, g_final, 0, Tp, TOKEN_TILE, True).reshape(Bp, Lp, D_MODEL)
            y_sample = _combine(x1, y, rw, g_final, Tp, Ts, TOKEN_TILE, True).reshape(Bs, Ls, D_MODEL)
        gla_p.append(a1)
        gla_s.append(b1)
        conv_p.append(a2)
        conv_s.append(b2)
        k_p = sk[:Tp].reshape(Bp, Lp, kvw)
        v_p = sv[:Tp].reshape(Bp, Lp, kvw)
        kp.append(k_p[:, -keep:].reshape(Bp, keep, SWA_HKV, SWA_DH))
        vp.append(v_p[:, -keep:].reshape(Bp, keep, SWA_HKV, SWA_DH))
        k_s = sk[Tp:].reshape(Bs, Ls, kvw)
        v_s = sv[Tp:].reshape(Bs, Ls, kvw)
        ks_.append(jnp.concatenate([ck, k_s], axis=1)[:, -nbuf:].reshape(Bs, nbuf, SWA_HKV, SWA_DH))
        vs.append(jnp.concatenate([cv, v_s], axis=1)[:, -nbuf:].reshape(Bs, nbuf, SWA_HKV, SWA_DH))
    return (y_prompt, y_sample, jnp.stack(gla_p), jnp.stack(gla_s), jnp.stack(conv_p), jnp.stack(conv_s),
            jnp.stack(kp), jnp.stack(ks_), jnp.stack(vp), jnp.stack(vs))
```

```python
import functools

import numpy as np
import jax
import jax.numpy as jnp
from jax import lax
from jax.experimental import pallas as pl
from jax.experimental.pallas import tpu as pltpu

F32 = jnp.float32
BF16 = jnp.bfloat16

D_MODEL = 1024
BRANCH_W = 512
GLA_HEADS = 4
GLA_DK = 64
GLA_DV = 128
GLA_KW = GLA_HEADS * GLA_DK
GLA_VW = GLA_HEADS * GLA_DV
GATE_RANK = 16
GATE_NORMALIZER = 16.0
GLA_CHUNK = 64
CONV_C = 512
CONV_K = 31
SWA_HQ = 8
SWA_HKV = 2
SWA_G = SWA_HQ // SWA_HKV
SWA_DH = 64
WINDOW = 128
N_GROUPS = 4
EXPERTS_PER_GROUP = 8
N_EXPERTS = 32
EXPERT_F = 256
EPS = 1e-6
PAST_LEN = 16384

LANES = 128
VMEM_LIMIT_BYTES = 56 * 2**20
ROUTER_LANES = LANES
EXPERT_LANE0 = N_GROUPS
CONV_HIST = 32
MOE_TM = 256
TOP_K = 2
TOKEN_TILE = 512
SEQ_TILE = 256


def _mm(a, b):
    return jnp.dot(a, b, preferred_element_type=F32)


def _mm_nt(a, b):
    return lax.dot_general(a, b, (((1,), (1,)), ((), ())), preferred_element_type=F32)


def _mm_tn(a, b):
    return lax.dot_general(a, b, (((0,), (0,)), ((), ())), preferred_element_type=F32)


def _split2(x):
    hi = x.astype(BF16)
    lo = (x - hi.astype(F32)).astype(BF16)
    return hi, lo


def _split3(x):
    hi = x.astype(BF16)
    r = x - hi.astype(F32)
    mid = r.astype(BF16)
    lo = (r - mid.astype(F32)).astype(BF16)
    return hi, mid, lo


def _const_spec(shape):
    nd = len(shape)
    return pl.BlockSpec(shape, lambda *_: (0,) * nd)


def _params(*sem):
    return pltpu.CompilerParams(dimension_semantics=sem, vmem_limit_bytes=VMEM_LIMIT_BYTES)


def _rms(x):
    return lax.rsqrt(jnp.mean(x * x, axis=-1, keepdims=True) + EPS)


def _log_sigmoid(z):
    return jnp.minimum(z, 0.0) - jnp.log1p(jnp.exp(-jnp.abs(z)))


def _in_proj_body(x_ref, g_ref, wqkvg_ref, wfr_ref, wf2_ref, bf2_ref, wcglu_ref, wswa_ref, wmg_ref,
                  qkvg_ref, la_ref, u_ref, sq_ref, sk_ref, sv_ref, gates_ref):
    x = x_ref[...]
    h = ((x * _rms(x)) * g_ref[...]).astype(BF16)
    qkvg_ref[...] = _mm(h, wqkvg_ref[...])
    gfr_hi, gfr_lo = _split2(_mm(h, wfr_ref[...]))
    z = _mm(gfr_hi, wf2_ref[0]) + (_mm(gfr_lo, wf2_ref[0]) + _mm(gfr_hi, wf2_ref[1])) + bf2_ref[...]
    la_ref[...] = _log_sigmoid(z) * (1.0 / GATE_NORMALIZER)
    cglu = _mm(h, wcglu_ref[...])
    u_ref[...] = cglu[:, :CONV_C] * jax.nn.sigmoid(cglu[:, CONV_C:])
    swa = _mm(h, wswa_ref[...])
    sq_ref[...] = swa[:, :SWA_HQ * SWA_DH] * (SWA_DH ** -0.5)
    sk_ref[...] = swa[:, SWA_HQ * SWA_DH:SWA_HQ * SWA_DH + SWA_HKV * SWA_DH]
    sv_ref[...] = swa[:, SWA_HQ * SWA_DH + SWA_HKV * SWA_DH:]
    gates_ref[...] = jax.nn.sigmoid(_mm(h, wmg_ref[...])).astype(BF16)


def _in_proj(x, lw, tm):
    T = x.shape[0]
    row = lambda n: pl.BlockSpec((tm, n), lambda i: (i, 0))
    outs = [(GLA_KW * 2 + GLA_VW * 2, F32), (GLA_KW, F32), (CONV_C, F32), (SWA_HQ * SWA_DH, F32),
            (SWA_HKV * SWA_DH, F32), (SWA_HKV * SWA_DH, F32), (3 * D_MODEL, BF16)]
    consts = [lw['norm_mix_g'], lw['w_qkvg'], lw['w_fr'], lw['w_f2'], lw['b_f2'], lw['w_cglu'], lw['w_swa'],
              lw['w_mg']]
    return pl.pallas_call(
        _in_proj_body,
        grid=(T // tm,),
        in_specs=[row(D_MODEL)] + [_const_spec(c.shape) for c in consts],
        out_specs=[row(n) for n, _ in outs],
        out_shape=[jax.ShapeDtypeStruct((T, n), dt) for n, dt in outs],
        compiler_params=_params("parallel"),
        name="in_proj",
    )(x, *consts)


def _gla_levels(C):
    return [w for w in (32, 16, 8, 4, 2, 1) if w < C]


def _gla_sum_matrix(C):
    t = np.arange(C)
    mats = [(t[:, None] >= t[None, :]).astype(np.float32)]
    qs, ks = [], []
    for w in _gla_levels(C):
        start = (t // (2 * w)) * (2 * w)
        upper = (t % (2 * w)) >= w
        mq = upper[:, None] & (t[None, :] >= (start + w)[:, None]) & (t[None, :] <= t[:, None])
        mk = (~upper)[:, None] & (t[None, :] > t[:, None]) & (t[None, :] <= (start + w - 1)[:, None])
        qs.append(mq.astype(np.float32))
        ks.append(mk.astype(np.float32))
    return np.concatenate(mats + qs + ks, axis=0)


def _gla_body(qkvg_ref, la_ref, s0_ref, m_ref, gng_ref, o_ref, s_out_ref, st_ref, *, C, TT, exact):
    levels = _gla_levels(C)
    n = len(levels)
    mmdt = F32 if exact else BF16
    step = pl.program_id(1)

    @pl.when(step == 0)
    def _():
        for h in range(GLA_HEADS):
            st_ref[h] = s0_ref[h].T

    def chunk(c, carry):
        rows = pl.ds(pl.multiple_of(c * C, C), C)
        qkvg = qkvg_ref[rows, :]
        la = la_ref[rows, :]
        if exact:
            e_all = jnp.dot(m_ref[...], la, precision=lax.Precision.HIGHEST, preferred_element_type=F32)
        else:
            e_all = _mm(m_ref[...], jnp.concatenate(_split3(la), axis=0))
        b = e_all[0:C]
        q = qkvg[:, 0:GLA_KW] * (GLA_DK ** -0.5)
        k = qkvg[:, GLA_KW:2 * GLA_KW]
        v = qkvg[:, 2 * GLA_KW:2 * GLA_KW + GLA_VW]
        g = qkvg[:, 2 * GLA_KW + GLA_VW:]
        rowi = lax.broadcasted_iota(jnp.int32, (C, GLA_KW), 0)
        qs, ks = [], []
        for li, w in enumerate(levels):
            eq = e_all[(1 + li) * C:(2 + li) * C]
            ek = e_all[(1 + n + li) * C:(2 + n + li) * C]
            upper = (rowi & w) != 0
            qs.append(jnp.where(upper, q * jnp.exp(eq), 0.0).astype(mmdt))
            ks.append(jnp.where(upper, 0.0, k * jnp.exp(ek)).astype(mmdt))
        qb = (q * jnp.exp(b)).astype(mmdt)
        b_last = b[C - 1:C, :]
        kd = (k * jnp.exp(b_last - b)).astype(mmdt)
        dec = jnp.exp(b_last)
        qm = q.astype(mmdt)
        km = k.astype(mmdt)
        ti = lax.broadcasted_iota(jnp.int32, (C, C), 0)
        si = lax.broadcasted_iota(jnp.int32, (C, C), 1)
        for h in range(GLA_HEADS):
            hl = slice(h * GLA_DK, (h + 1) * GLA_DK)
            vl = slice(h * GLA_DV, (h + 1) * GLA_DV)
            attn = jnp.where(ti == si, _mm_nt(qm[:, hl], km[:, hl]), 0.0)
            for li, w in enumerate(levels):
                p = _mm_nt(qs[li][:, hl], ks[li][:, hl])
                if 2 * w < C:
                    sh = (2 * w).bit_length() - 1
                    p = jnp.where(lax.shift_right_logical(ti, sh) == lax.shift_right_logical(si, sh), p, 0.0)
                attn = attn + p
            vh = v[:, vl].astype(mmdt)
            st = st_ref[h]
            o = _mm(attn.astype(mmdt), vh) + _mm_nt(qb[:, hl], st.astype(mmdt))
            st_ref[h] = st * dec[:, hl] + _mm_tn(vh, kd[:, hl])
            gh = g[:, vl]
            o = ((o * _rms(o)) * gng_ref[...]) * (gh * jax.nn.sigmoid(gh))
            o_ref[rows, vl] = o.astype(o_ref.dtype)
        return carry

    if TT == C:
        chunk(0, 0)
    else:
        lax.fori_loop(0, TT // C, chunk, 0)

    @pl.when(step == pl.num_programs(1) - 1)
    def _():
        for h in range(GLA_HEADS):
            s_out_ref[h] = st_ref[h].T


def _gla(qkvg, la, s0, gng, B, L, row0, TT):
    C = min(GLA_CHUNK, L)
    TT = min(TT, L)
    nT = L // TT
    off = row0 // TT
    exact = C < 16
    m = _gla_sum_matrix(C)
    m = jnp.asarray(m, F32) if exact else jnp.asarray(np.concatenate([m, m, m], axis=1), BF16)
    body = functools.partial(_gla_body, C=C, TT=TT, exact=exact)
    return pl.pallas_call(
        body,
        grid=(B, nT),
        in_specs=[pl.BlockSpec((TT, qkvg.shape[1]), lambda b, i: (off + b * nT + i, 0)),
                  pl.BlockSpec((TT, GLA_KW), lambda b, i: (off + b * nT + i, 0)),
                  pl.BlockSpec((None, GLA_HEADS, GLA_DK, GLA_DV), lambda b, i: (b, 0, 0, 0)),
                  _const_spec(m.shape), _const_spec(gng.shape)],
        out_specs=[pl.BlockSpec((TT, GLA_VW), lambda b, i: (b * nT + i, 0)),
                   pl.BlockSpec((None, GLA_HEADS, GLA_DK, GLA_DV), lambda b, i: (b, 0, 0, 0))],
        out_shape=[jax.ShapeDtypeStruct((B * L, GLA_VW), BF16),
                   jax.ShapeDtypeStruct((B, GLA_HEADS, GLA_DK, GLA_DV), F32)],
        scratch_shapes=[pltpu.VMEM((GLA_HEADS, GLA_DV, GLA_DK), F32)],
        compiler_params=_params("parallel", "arbitrary"),
        name="gla",
    )(qkvg, la, s0, m, gng)


def _conv_body(u_ref, buf_ref, w_ref, cb_ref, lg_ref, lb_ref, o_ref, new_ref, ext_ref, *, TT):
    lo = CONV_HIST - (CONV_K - 1)

    @pl.when(pl.program_id(1) == 0)
    def _():
        ext_ref[lo:CONV_HIST, :] = buf_ref[...]

    ext_ref[CONV_HIST:CONV_HIST + TT, :] = u_ref[...]
    acc = jnp.zeros((TT, CONV_C), F32) + cb_ref[...]
    for j in range(CONV_K):
        acc = acc + w_ref[j:j + 1, :] * ext_ref[lo + j:lo + j + TT, :]
    mu = jnp.mean(acc, axis=-1, keepdims=True)
    d = acc - mu
    var = jnp.mean(d * d, axis=-1, keepdims=True)
    y = (d * lax.rsqrt(var + EPS)) * lg_ref[...] + lb_ref[...]
    o_ref[...] = (y * jax.nn.sigmoid(y)).astype(o_ref.dtype)
    tail = ext_ref[TT + lo:TT + CONV_HIST, :]
    new_ref[...] = tail
    ext_ref[lo:CONV_HIST, :] = tail


def _conv(u, buf, w, cb, lg, lb, B, L, row0, TT):
    TT = min(TT, L)
    nT = L // TT
    off = row0 // TT
    body = functools.partial(_conv_body, TT=TT)
    return pl.pallas_call(
        body,
        grid=(B, nT),
        in_specs=[pl.BlockSpec((TT, CONV_C), lambda b, i: (off + b * nT + i, 0)),
                  pl.BlockSpec((None, CONV_K - 1, CONV_C), lambda b, i: (b, 0, 0)),
                  _const_spec(w.shape), _const_spec(cb.shape), _const_spec(lg.shape), _const_spec(lb.shape)],
        out_specs=[pl.BlockSpec((TT, CONV_C), lambda b, i: (b * nT + i, 0)),
                   pl.BlockSpec((None, CONV_K - 1, CONV_C), lambda b, i: (b, 0, 0))],
        out_shape=[jax.ShapeDtypeStruct((B * L, CONV_C), BF16),
                   jax.ShapeDtypeStruct((B, CONV_K - 1, CONV_C), F32)],
        scratch_shapes=[pltpu.VMEM((CONV_HIST + TT, CONV_C), F32)],
        compiler_params=_params("parallel", "arbitrary"),
        name="conv",
    )(u, buf, w, cb, lg, lb)


def _swa_body(q_ref, kp_ref, kc_ref, vp_ref, vc_ref, sink_ref, o_ref, *, TQ, first_has_history, mmdt):
    q = q_ref[...].astype(mmdt)
    kp = kp_ref[...].astype(mmdt)
    kc = kc_ref[...].astype(mmdt)
    vp = vp_ref[...].astype(mmdt)
    vc = vc_ref[...].astype(mmdt)
    i1 = lax.broadcasted_iota(jnp.int32, (TQ, WINDOW), 0)
    j1 = lax.broadcasted_iota(jnp.int32, (TQ, WINDOW), 1)
    ok1 = j1 >= i1
    if not first_has_history:
        ok1 = jnp.logical_and(ok1, pl.program_id(1) > 0)
    d1 = (WINDOW + i1 - j1).astype(F32)
    i2 = lax.broadcasted_iota(jnp.int32, (TQ, TQ), 0)
    j2 = lax.broadcasted_iota(jnp.int32, (TQ, TQ), 1)
    ok2 = j2 <= i2
    d2 = (i2 - j2).astype(F32)
    for h in range(SWA_HQ):
        ql = slice(h * SWA_DH, (h + 1) * SWA_DH)
        kl = slice((h // SWA_G) * SWA_DH, (h // SWA_G + 1) * SWA_DH)
        slope = 2.0 ** (-8.0 * (h + 1) / SWA_HQ)
        s1 = jnp.where(ok1, _mm_nt(q[:, ql], kp[:, kl]) - slope * d1, -jnp.inf)
        s2 = jnp.where(ok2, _mm_nt(q[:, ql], kc[:, kl]) - slope * d2, -jnp.inf)
        sink = sink_ref[h]
        m = jnp.maximum(jnp.maximum(jnp.max(s1, -1, keepdims=True), jnp.max(s2, -1, keepdims=True)), sink)
        p1 = jnp.exp(s1 - m)
        p2 = jnp.exp(s2 - m)
        den = jnp.sum(p1, -1, keepdims=True) + jnp.sum(p2, -1, keepdims=True) + jnp.exp(sink - m)
        o = (_mm(p1.astype(mmdt), vp[:, kl]) + _mm(p2.astype(mmdt), vc[:, kl])) / den
        o_ref[:, ql] = o.astype(o_ref.dtype)


def _swa(q, k, v, kprev, vprev, sinks, B, L, row0):
    TQ = min(WINDOW, L)
    nT = L // TQ
    off = row0 // TQ
    kvw = SWA_HKV * SWA_DH
    cur = pl.BlockSpec((TQ, kvw), lambda b, n: (off + b * nT + n, 0))
    from_start = kprev is None
    if from_start:
        assert TQ == WINDOW
        prev = pl.BlockSpec((WINDOW, kvw), lambda b, n: (off + b * nT + jnp.maximum(n - 1, 0), 0))
        kprev, vprev = k, v
    else:
        assert L == TQ
        prev = pl.BlockSpec((None, WINDOW, kvw), lambda b, n: (b, 0, 0))
    body = functools.partial(_swa_body, TQ=TQ, first_has_history=not from_start, mmdt=BF16 if TQ >= 16 else F32)
    return pl.pallas_call(
        body,
        grid=(B, nT),
        in_specs=[pl.BlockSpec((TQ, SWA_HQ * SWA_DH), lambda b, n: (off + b * nT + n, 0)),
                  prev, cur, prev, cur, pl.BlockSpec(memory_space=pltpu.SMEM)],
        out_specs=pl.BlockSpec((TQ, SWA_HQ * SWA_DH), lambda b, n: (b * nT + n, 0)),
        out_shape=jax.ShapeDtypeStruct((B * L, SWA_HQ * SWA_DH), BF16),
        compiler_params=_params("parallel", "arbitrary"),
        name="swa",
    )(q, kprev, k, vprev, v, sinks)


def _route(logits):
    lane = lax.broadcasted_iota(jnp.int32, logits.shape, 1)
    big = jnp.int32(2 * ROUTER_LANES)
    is_g = lane < N_GROUPS
    gl = jnp.where(is_g, logits, -jnp.inf)
    gmax = jnp.max(gl, -1, keepdims=True)
    gsel = jnp.min(jnp.where(gl == gmax, lane, big), -1, keepdims=True)
    gprob = 1.0 / jnp.sum(jnp.where(is_g, jnp.exp(logits - gmax), 0.0), -1, keepdims=True)
    e = lane - EXPERT_LANE0
    in_grp = jnp.logical_and(jnp.logical_and(e >= 0, e < N_EXPERTS),
                             lax.shift_right_logical(jnp.maximum(e, 0), 3) == gsel)
    el = jnp.where(in_grp, logits, -jnp.inf)
    v1 = jnp.max(el, -1, keepdims=True)
    i1 = jnp.min(jnp.where(el == v1, lane, big), -1, keepdims=True)
    el2 = jnp.where(lane == i1, -jnp.inf, el)
    v2 = jnp.max(el2, -1, keepdims=True)
    i2 = jnp.min(jnp.where(el2 == v2, lane, big), -1, keepdims=True)
    t = jnp.exp(v2 - v1)
    w1 = gprob / (1.0 + t)
    w2 = gprob * t / (1.0 + t)
    return lane, i1, i2, w1, w2


def _merge_body(og_ref, oc_ref, os_ref, gates_ref, x_ref, wb_ref, wo_ref, g2_ref, wr_ref, br_ref, tri_ref,
                x1_ref, hf_ref, ridx_ref, rw_ref, cnt_ref, seen_ref):
    @pl.when(pl.program_id(0) == 0)
    def _():
        seen_ref[...] = jnp.zeros_like(seen_ref)

    y = (gates_ref[:, 0:D_MODEL].astype(F32) * _mm(og_ref[...], wb_ref[0])
         + gates_ref[:, D_MODEL:2 * D_MODEL].astype(F32) * _mm(oc_ref[...], wb_ref[1])
         + gates_ref[:, 2 * D_MODEL:].astype(F32) * _mm(os_ref[...], wb_ref[2]))
    x1 = x_ref[...] + _mm(y.astype(BF16), wo_ref[...])
    x1_ref[...] = x1
    hf = (x1 * _rms(x1)) * g2_ref[...]
    hf_ref[...] = hf
    h_hi, h_lo = _split2(hf)
    logits = _mm(h_hi, wr_ref[0]) + (_mm(h_lo, wr_ref[0]) + _mm(h_hi, wr_ref[1])) + br_ref[...]
    lane, i1, i2, w1, w2 = _route(logits)
    onehot = jnp.where(jnp.logical_or(lane == i1, lane == i2), 1.0, 0.0)
    before = _mm(tri_ref[...], onehot.astype(BF16)) + seen_ref[...]
    r1 = jnp.sum(jnp.where(lane == i1, before, 0.0), -1, keepdims=True).astype(jnp.int32)
    r2 = jnp.sum(jnp.where(lane == i2, before, 0.0), -1, keepdims=True).astype(jnp.int32)
    seen = seen_ref[...] + jnp.sum(onehot, axis=0, keepdims=True)
    seen_ref[...] = seen
    cnt_ref[...] = jnp.broadcast_to(seen, cnt_ref.shape)
    ridx_ref[...] = jnp.where(lane == 0, i1 - EXPERT_LANE0,
                              jnp.where(lane == 1, i2 - EXPERT_LANE0,
                                        jnp.where(lane == 2, r1, jnp.where(lane == 3, r2, 0))))
    rw_ref[...] = jnp.where(lane == 0, w1, jnp.where(lane == 1, w2, 0.0))


def _merge(og, oc, osw, gates, x, lw, tm):
    T = x.shape[0]
    row = lambda n: pl.BlockSpec((tm, n), lambda i: (i, 0))
    tri = jnp.asarray(np.tril(np.ones((tm, tm), np.float32), -1), BF16)
    consts = [lw['w_branch'], lw['w_out'], lw['norm_ffn_g'], lw['w_router'], lw['b_router'], tri]
    return pl.pallas_call(
        _merge_body,
        grid=(T // tm,),
        in_specs=[row(BRANCH_W), row(BRANCH_W), row(BRANCH_W), row(3 * D_MODEL), row(D_MODEL)]
                 + [_const_spec(c.shape) for c in consts],
        out_specs=[row(D_MODEL), row(D_MODEL), row(ROUTER_LANES), row(ROUTER_LANES),
                   _const_spec((8, ROUTER_LANES))],
        out_shape=[jax.ShapeDtypeStruct((T, D_MODEL), F32), jax.ShapeDtypeStruct((T, D_MODEL), F32),
                   jax.ShapeDtypeStruct((T, ROUTER_LANES), jnp.int32),
                   jax.ShapeDtypeStruct((T, ROUTER_LANES), F32),
                   jax.ShapeDtypeStruct((8, ROUTER_LANES), F32)],
        scratch_shapes=[pltpu.VMEM((1, ROUTER_LANES), F32)],
        compiler_params=_params("arbitrary"),
        name="merge",
    )(og, oc, osw, gates, x, *consts)


def _moe_tiles(T):
    return -(-TOP_K * T // MOE_TM) + N_EXPERTS


def _dispatch_plan(ridx, cnt, T):
    NT = _moe_tiles(T)
    n_rows = NT * MOE_TM
    counts = cnt[0, EXPERT_LANE0:EXPERT_LANE0 + N_EXPERTS].astype(jnp.int32)
    padded = ((counts + MOE_TM - 1) // MOE_TM) * MOE_TM
    ends = jnp.cumsum(padded)
    start = ends - padded
    tok = jnp.arange(T, dtype=jnp.int32)
    d1 = start[ridx[:, 0]] + ridx[:, 2]
    d2 = start[ridx[:, 1]] + ridx[:, 3]
    row = jnp.arange(n_rows, dtype=jnp.int32)
    spare = TOP_K * T + ((row // MOE_TM) % 2) * MOE_TM + row % MOE_TM
    code = jnp.full((n_rows,), -1, jnp.int32).at[jnp.concatenate([d1, d2])].set(
        jnp.concatenate([2 * tok, 2 * tok + 1]), unique_indices=True)
    src = jnp.maximum(code, 0) // 2
    dst = jnp.where(code < 0, spare, (code % 2) * T + code // 2)
    tile_end = ends // MOE_TM
    n_used = tile_end[-1:]
    tile_expert = jnp.searchsorted(tile_end, jnp.minimum(jnp.arange(NT, dtype=jnp.int32), n_used - 1),
                                   side='right', method='compare_all').astype(jnp.int32)
    return src.reshape(NT, 1, MOE_TM), dst.reshape(NT, 1, MOE_TM), tile_expert, n_used.astype(jnp.int32)


def _ffn_body(te_ref, nu_ref, src_next_ref, src_first_ref, dst_ref, h_hbm, wg_ref, wu_ref, wd_ref, y_hbm,
              xbuf, ybuf, gsem, ssem):
    j = pl.program_id(0)
    last = pl.num_programs(0) - 1
    nu = nu_ref[0]
    slot = lax.rem(j, 2)

    def start_gather(idx_ref, s):
        def body(r, c):
            pltpu.make_async_copy(h_hbm.at[pl.ds(idx_ref[0, 0, r], 1)], xbuf.at[s, pl.ds(r, 1)], gsem.at[s]).start()
            return c
        lax.fori_loop(0, MOE_TM, body, 0, unroll=8)

    def wait_gather(s):
        pltpu.make_async_copy(h_hbm.at[pl.ds(0, MOE_TM)], xbuf.at[s], gsem.at[s]).wait()

    def wait_scatter(s):
        pltpu.make_async_copy(ybuf.at[s], y_hbm.at[pl.ds(0, MOE_TM)], ssem.at[s]).wait()

    @pl.when(j == 0)
    def _():
        start_gather(src_first_ref, 0)
        ybuf[...] = jnp.zeros_like(ybuf)
        spare0 = y_hbm.shape[0] - 2 * MOE_TM
        for s in range(2):
            init = pltpu.make_async_copy(ybuf.at[s], y_hbm.at[pl.ds(spare0 + s * MOE_TM, MOE_TM)], ssem.at[s])
            init.start()
            init.wait()

    @pl.when(jnp.logical_and(j >= 2, j - 2 < nu))
    def _():
        wait_scatter(slot)

    @pl.when(j < nu)
    def _():
        wait_gather(slot)

        @pl.when(j + 1 < nu)
        def _():
            start_gather(src_next_ref, 1 - slot)

        x = xbuf[slot].astype(BF16)
        a = _mm(x, wg_ref[...])
        u = _mm(x, wu_ref[...])
        hid = (a * jax.nn.sigmoid(a)) * u
        ybuf[slot] = _mm(hid.astype(BF16), wd_ref[...])

        def body(r, c):
            pltpu.make_async_copy(ybuf.at[slot, pl.ds(r, 1)], y_hbm.at[pl.ds(dst_ref[0, 0, r], 1)],
                                  ssem.at[slot]).start()
            return c
        lax.fori_loop(0, MOE_TM, body, 0, unroll=8)

    @pl.when(j == last)
    def _():
        @pl.when(jnp.logical_and(j >= 1, j - 1 < nu))
        def _():
            wait_scatter(1 - slot)

        @pl.when(j < nu)
        def _():
            wait_scatter(slot)


def _ffn(hf, plan, lw):
    T = hf.shape[0]
    src, dst, tile_expert, n_used = plan
    NT = src.shape[0]
    idx = lambda f: pl.BlockSpec((1, 1, MOE_TM), f, memory_space=pltpu.SMEM)
    wspec = lambda shape: pl.BlockSpec((None,) + shape, lambda j, te, nu: (te[j], 0, 0))
    grid_spec = pltpu.PrefetchScalarGridSpec(
        num_scalar_prefetch=2,
        grid=(NT,),
        in_specs=[idx(lambda j, te, nu: (jnp.minimum(j + 1, NT - 1), 0, 0)),
                  idx(lambda j, te, nu: (0, 0, 0)),
                  idx(lambda j, te, nu: (j, 0, 0)),
                  pl.BlockSpec(memory_space=pl.ANY),
                  wspec((D_MODEL, EXPERT_F)), wspec((D_MODEL, EXPERT_F)), wspec((EXPERT_F, D_MODEL))],
        out_specs=pl.BlockSpec(memory_space=pl.ANY),
        scratch_shapes=[pltpu.VMEM((2, MOE_TM, D_MODEL), F32), pltpu.VMEM((2, MOE_TM, D_MODEL), F32),
                        pltpu.SemaphoreType.DMA((2,)), pltpu.SemaphoreType.DMA((2,))])
    return pl.pallas_call(
        _ffn_body,
        grid_spec=grid_spec,
        out_shape=jax.ShapeDtypeStruct((TOP_K * T + 2 * MOE_TM, D_MODEL), F32),
        compiler_params=_params("arbitrary"),
        name="ffn",
    )(tile_expert, n_used, src, src, dst, hf, lw['w_exp_gate'], lw['w_exp_up'], lw['w_exp_down'])


def _combine_body(x1_ref, ya_ref, yb_ref, rw_ref, g_ref, o_ref, *, final):
    rw = rw_ref[...]
    lane = lax.broadcasted_iota(jnp.int32, rw.shape, 1)
    w1 = jnp.sum(jnp.where(lane == 0, rw, 0.0), -1, keepdims=True)
    w2 = jnp.sum(jnp.where(lane == 1, rw, 0.0), -1, keepdims=True)
    x2 = x1_ref[...] + (w1 * ya_ref[...] + w2 * yb_ref[...])
    if final:
        x2 = (x2 * _rms(x2)) * g_ref[...]
    o_ref[...] = x2


def _combine(x1, y, rw, g, row0, rows, tm, final):
    T = x1.shape[0]
    off = row0 // tm
    body = functools.partial(_combine_body, final=final)
    return pl.pallas_call(
        body,
        grid=(rows // tm,),
        in_specs=[pl.BlockSpec((tm, D_MODEL), lambda i: (off + i, 0)),
                  pl.BlockSpec((tm, D_MODEL), lambda i: (off + i, 0)),
                  pl.BlockSpec((tm, D_MODEL), lambda i: (T // tm + off + i, 0)),
                  pl.BlockSpec((tm, ROUTER_LANES), lambda i: (off + i, 0)),
                  _const_spec(g.shape)],
        out_specs=pl.BlockSpec((tm, D_MODEL), lambda i: (i, 0)),
        out_shape=jax.ShapeDtypeStruct((rows, D_MODEL), F32),
        compiler_params=_params("parallel"),
        name="combine",
    )(x1, y, y, rw, g)


def _prep_layer(l, norm_mix_g, w_in, w_fgate2, b_fgate2, gla_norm_g, conv_w, conv_b, conv_ln_g, conv_ln_b,
                swa_sinks, w_branch, w_out, norm_ffn_g, w_router_group, b_router_group, w_router_expert,
                b_router_expert, w_exp_gate, w_exp_up, w_exp_down):
    w = w_in[l]
    c0 = 2 * GLA_KW + 2 * GLA_VW
    c1 = c0 + GATE_RANK
    c2 = c1 + 2 * CONV_C
    c3 = c2 + SWA_HQ * SWA_DH + 2 * SWA_HKV * SWA_DH
    pad_r = LANES - GATE_RANK
    wf2 = jnp.pad(w_fgate2[l], ((0, pad_r), (0, 0)))
    wr = jnp.pad(jnp.concatenate([w_router_group[l], w_router_expert[l]], axis=1),
                 ((0, 0), (0, ROUTER_LANES - N_GROUPS - N_EXPERTS)))
    br = jnp.pad(jnp.concatenate([b_router_group[l], b_router_expert[l]]),
                 (0, ROUTER_LANES - N_GROUPS - N_EXPERTS))
    return {
        'norm_mix_g': norm_mix_g[l][None, :],
        'w_qkvg': w[:, :c0].astype(BF16),
        'w_fr': jnp.pad(w[:, c0:c1], ((0, 0), (0, pad_r))).astype(BF16),
        'w_f2': jnp.stack(_split2(wf2)),
        'b_f2': b_fgate2[l][None, :],
        'w_cglu': w[:, c1:c2].astype(BF16),
        'w_swa': w[:, c2:c3].astype(BF16),
        'w_mg': w[:, c3:].astype(BF16),
        'gla_norm_g': gla_norm_g[l][None, :],
        'conv_w': conv_w[l],
        'conv_b': conv_b[l][None, :],
        'conv_ln_g': conv_ln_g[l][None, :],
        'conv_ln_b': conv_ln_b[l][None, :],
        'swa_sinks': swa_sinks[l],
        'w_branch': w_branch[l].reshape(3, BRANCH_W, D_MODEL).astype(BF16),
        'w_out': w_out[l].astype(BF16),
        'norm_ffn_g': norm_ffn_g[l][None, :],
        'w_router': jnp.stack(_split2(wr)),
        'b_router': br[None, :],
        'w_exp_gate': w_exp_gate[l].astype(BF16),
        'w_exp_up': w_exp_up[l].astype(BF16),
        'w_exp_down': w_exp_down[l].astype(BF16),
    }


def kernel(x_prompt, x_sample, state_gla, cache_conv, cache_swa_k, cache_swa_v, norm_mix_g, w_in, w_fgate2, b_fgate2, gla_norm_g, conv_w, conv_b, conv_ln_g, conv_ln_b, swa_sinks, w_branch, w_out, norm_ffn_g, w_router_group, b_router_group, w_router_expert, b_router_expert, w_exp_gate, w_exp_up, w_exp_down, final_norm_g):
    depth = w_in.shape[0]
    Bp, Lp, _ = x_prompt.shape
    Bs, Ls, _ = x_sample.shape
    Tp, Ts = Bp * Lp, Bs * Ls
    T = Tp + Ts
    nbuf = cache_swa_k.shape[2]
    assert nbuf == WINDOW and PAST_LEN >= nbuf
    kvw = SWA_HKV * SWA_DH
    x = jnp.concatenate([x_prompt.reshape(Tp, D_MODEL), x_sample.reshape(Ts, D_MODEL)], axis=0)
    gla_p, gla_s, conv_p, conv_s, kp, ks_, vp, vs = [], [], [], [], [], [], [], []
    keep = min(WINDOW, Lp)
    g_final = final_norm_g[None, :]
    for l in range(depth):
        lw = _prep_layer(l, norm_mix_g, w_in, w_fgate2, b_fgate2, gla_norm_g, conv_w, conv_b, conv_ln_g,
                         conv_ln_b, swa_sinks, w_branch, w_out, norm_ffn_g, w_router_group, b_router_group,
                         w_router_expert, b_router_expert, w_exp_gate, w_exp_up, w_exp_down)
        qkvg, la, u, sq, sk, sv, gates = _in_proj(x, lw, TOKEN_TILE)
        s0 = jnp.zeros((Bp, GLA_HEADS, GLA_DK, GLA_DV), F32)
        cb0 = jnp.zeros((Bp, CONV_K - 1, CONV_C), F32)
        og_p, a1 = _gla(qkvg, la, s0, lw['gla_norm_g'], Bp, Lp, 0, SEQ_TILE)
        oc_p, a2 = _conv(u, cb0, lw['conv_w'], lw['conv_b'], lw['conv_ln_g'], lw['conv_ln_b'], Bp, Lp, 0, SEQ_TILE)
        os_p = _swa(sq, sk, sv, None, None, lw['swa_sinks'], Bp, Lp, 0)
        ck = cache_swa_k[l].reshape(Bs, nbuf, kvw)
        cv = cache_swa_v[l].reshape(Bs, nbuf, kvw)
        og_s, b1 = _gla(qkvg, la, state_gla[l], lw['gla_norm_g'], Bs, Ls, Tp, SEQ_TILE)
        oc_s, b2 = _conv(u, cache_conv[l], lw['conv_w'], lw['conv_b'], lw['conv_ln_g'], lw['conv_ln_b'],
                         Bs, Ls, Tp, SEQ_TILE)
        os_s = _swa(sq, sk, sv, ck, cv, lw['swa_sinks'], Bs, Ls, Tp)
        cat = lambda a, b: jnp.concatenate([a, b], axis=0)
        x1, hf, ridx, rw, cnt = _merge(cat(og_p, og_s), cat(oc_p, oc_s), cat(os_p, os_s), gates, x, lw, TOKEN_TILE)
        y = _ffn(hf, _dispatch_plan(ridx, cnt, T), lw)
        if l + 1 < depth:
            x = _combine(x1, y, rw, g_final, 0, T, TOKEN_TILE, False)
        else:
            y_prompt = _combine(x1, y, rw, g_final, 0, Tp, TOKEN_TILE, True).reshape(Bp, Lp, D_MODEL)
            y_sample = _combine(x1, y, rw, g_final, Tp, Ts, TOKEN_TILE, True).reshape(Bs, Ls, D_MODEL)
        gla_p.append(a1)
        gla_s.append(b1)
        conv_p.append(a2)
        conv_s.append(b2)
        k_p = sk[:Tp].reshape(Bp, Lp, kvw)
        v_p = sv[:Tp].reshape(Bp, Lp, kvw)
        kp.append(k_p[:, -keep:].reshape(Bp, keep, SWA_HKV, SWA_DH))
        vp.append(v_p[:, -keep:].reshape(Bp, keep, SWA_HKV, SWA_DH))
        k_s = sk[Tp:].reshape(Bs, Ls, kvw)
        v_s = sv[Tp:].reshape(Bs, Ls, kvw)
        ks_.append(jnp.concatenate([ck, k_s], axis=1)[:, -nbuf:].reshape(Bs, nbuf, SWA_HKV, SWA_DH))
        vs.append(jnp.concatenate([cv, v_s], axis=1)[:, -nbuf:].reshape(Bs, nbuf, SWA_HKV, SWA_DH))
    return (y_prompt, y_sample, jnp.stack(gla_p), jnp.stack(gla_s), jnp.stack(conv_p), jnp.stack(conv_s),
            jnp.stack(kp), jnp.stack(ks_), jnp.stack(vp), jnp.stack(vs))
```

```python
import functools

import numpy as np
import jax
import jax.numpy as jnp
from jax import lax
from jax.experimental import pallas as pl
from jax.experimental.pallas import tpu as pltpu

F32 = jnp.float32
BF16 = jnp.bfloat16

D_MODEL = 1024
BRANCH_W = 512
GLA_HEADS = 4
GLA_DK = 64
GLA_DV = 128
GLA_KW = GLA_HEADS * GLA_DK
GLA_VW = GLA_HEADS * GLA_DV
GATE_RANK = 16
GATE_NORMALIZER = 16.0
GLA_CHUNK = 64
CONV_C = 512
CONV_K = 31
SWA_HQ = 8
SWA_HKV = 2
SWA_G = SWA_HQ // SWA_HKV
SWA_DH = 64
WINDOW = 128
N_GROUPS = 4
EXPERTS_PER_GROUP = 8
N_EXPERTS = 32
EXPERT_F = 256
EPS = 1e-6
PAST_LEN = 16384

LANES = 128
VMEM_LIMIT_BYTES = 56 * 2**20
ROUTER_LANES = LANES
EXPERT_LANE0 = N_GROUPS
CONV_HIST = 32
MOE_TM = 256
TOP_K = 2
TOKEN_TILE = 512
SEQ_TILE = 256
SWA_SEQS_PER_STEP = 8


def _mm(a, b):
    return jnp.dot(a, b, preferred_element_type=F32)


def _mm_nt(a, b):
    return lax.dot_general(a, b, (((1,), (1,)), ((), ())), preferred_element_type=F32)


def _mm_tn(a, b):
    return lax.dot_general(a, b, (((0,), (0,)), ((), ())), preferred_element_type=F32)


def _split2(x):
    hi = x.astype(BF16)
    lo = (x - hi.astype(F32)).astype(BF16)
    return hi, lo


def _split3(x):
    hi = x.astype(BF16)
    r = x - hi.astype(F32)
    mid = r.astype(BF16)
    lo = (r - mid.astype(F32)).astype(BF16)
    return hi, mid, lo


def _const_spec(shape):
    nd = len(shape)
    return pl.BlockSpec(shape, lambda *_: (0,) * nd)


def _params(*sem):
    return pltpu.CompilerParams(dimension_semantics=sem, vmem_limit_bytes=VMEM_LIMIT_BYTES)


def _rms(x):
    return lax.rsqrt(jnp.mean(x * x, axis=-1, keepdims=True) + EPS)


def _log_sigmoid(z):
    return jnp.minimum(z, 0.0) - jnp.log1p(jnp.exp(-jnp.abs(z)))


def _in_proj_body(x_ref, g_ref, wqkvg_ref, wfr_ref, wf2_ref, bf2_ref, wcglu_ref, wswa_ref, wmg_ref,
                  qkvg_ref, la_ref, u_ref, sq_ref, sk_ref, sv_ref, gates_ref):
    x = x_ref[...]
    h = ((x * _rms(x)) * g_ref[...]).astype(BF16)
    qkvg_ref[...] = _mm(h, wqkvg_ref[...])
    gfr_hi, gfr_lo = _split2(_mm(h, wfr_ref[...]))
    z = _mm(gfr_hi, wf2_ref[0]) + (_mm(gfr_lo, wf2_ref[0]) + _mm(gfr_hi, wf2_ref[1])) + bf2_ref[...]
    la_ref[...] = _log_sigmoid(z) * (1.0 / GATE_NORMALIZER)
    cglu = _mm(h, wcglu_ref[...])
    u_ref[...] = cglu[:, :CONV_C] * jax.nn.sigmoid(cglu[:, CONV_C:])
    swa = _mm(h, wswa_ref[...])
    sq_ref[...] = swa[:, :SWA_HQ * SWA_DH] * (SWA_DH ** -0.5)
    sk_ref[...] = swa[:, SWA_HQ * SWA_DH:SWA_HQ * SWA_DH + SWA_HKV * SWA_DH]
    sv_ref[...] = swa[:, SWA_HQ * SWA_DH + SWA_HKV * SWA_DH:]
    gates_ref[...] = jax.nn.sigmoid(_mm(h, wmg_ref[...])).astype(BF16)


def _in_proj(x, lw, tm):
    T = x.shape[0]
    row = lambda n: pl.BlockSpec((tm, n), lambda i: (i, 0))
    outs = [(GLA_KW * 2 + GLA_VW * 2, F32), (GLA_KW, F32), (CONV_C, F32), (SWA_HQ * SWA_DH, F32),
            (SWA_HKV * SWA_DH, F32), (SWA_HKV * SWA_DH, F32), (3 * D_MODEL, BF16)]
    consts = [lw['norm_mix_g'], lw['w_qkvg'], lw['w_fr'], lw['w_f2'], lw['b_f2'], lw['w_cglu'], lw['w_swa'],
              lw['w_mg']]
    return pl.pallas_call(
        _in_proj_body,
        grid=(T // tm,),
        in_specs=[row(D_MODEL)] + [_const_spec(c.shape) for c in consts],
        out_specs=[row(n) for n, _ in outs],
        out_shape=[jax.ShapeDtypeStruct((T, n), dt) for n, dt in outs],
        compiler_params=_params("parallel"),
        name="in_proj",
    )(x, *consts)


def _gla_levels(C):
    return [w for w in (32, 16, 8, 4, 2, 1) if w < C]


def _gla_sum_matrix(C):
    t = np.arange(C)
    mats = [(t[:, None] >= t[None, :]).astype(np.float32)]
    qs, ks = [], []
    for w in _gla_levels(C):
        start = (t // (2 * w)) * (2 * w)
        upper = (t % (2 * w)) >= w
        mq = upper[:, None] & (t[None, :] >= (start + w)[:, None]) & (t[None, :] <= t[:, None])
        mk = (~upper)[:, None] & (t[None, :] > t[:, None]) & (t[None, :] <= (start + w - 1)[:, None])
        qs.append(mq.astype(np.float32))
        ks.append(mk.astype(np.float32))
    return np.concatenate(mats + qs + ks, axis=0)


def _gla_body(qkvg_ref, la_ref, s0_ref, m_ref, gng_ref, o_ref, s_out_ref, st_ref, *, C, TT, exact):
    levels = _gla_levels(C)
    n = len(levels)
    mmdt = F32 if exact else BF16
    step = pl.program_id(1)

    @pl.when(step == 0)
    def _():
        for h in range(GLA_HEADS):
            st_ref[h] = s0_ref[h].T

    def chunk(c, carry):
        rows = pl.ds(pl.multiple_of(c * C, C), C)
        qkvg = qkvg_ref[rows, :]
        la = la_ref[rows, :]
        if exact:
            e_all = jnp.dot(m_ref[...], la, precision=lax.Precision.HIGHEST, preferred_element_type=F32)
        else:
            e_all = _mm(m_ref[...], jnp.concatenate(_split3(la), axis=0))
        b = e_all[0:C]
        q = qkvg[:, 0:GLA_KW] * (GLA_DK ** -0.5)
        k = qkvg[:, GLA_KW:2 * GLA_KW]
        v = qkvg[:, 2 * GLA_KW:2 * GLA_KW + GLA_VW]
        g = qkvg[:, 2 * GLA_KW + GLA_VW:]
        rowi = lax.broadcasted_iota(jnp.int32, (C, GLA_KW), 0)
        qs, ks = [], []
        for li, w in enumerate(levels):
            eq = e_all[(1 + li) * C:(2 + li) * C]
            ek = e_all[(1 + n + li) * C:(2 + n + li) * C]
            upper = (rowi & w) != 0
            qs.append(jnp.where(upper, q * jnp.exp(eq), 0.0).astype(mmdt))
            ks.append(jnp.where(upper, 0.0, k * jnp.exp(ek)).astype(mmdt))
        qb = (q * jnp.exp(b)).astype(mmdt)
        b_last = b[C - 1:C, :]
        kd = (k * jnp.exp(b_last - b)).astype(mmdt)
        dec = jnp.exp(b_last)
        qm = q.astype(mmdt)
        km = k.astype(mmdt)
        ti = lax.broadcasted_iota(jnp.int32, (C, C), 0)
        si = lax.broadcasted_iota(jnp.int32, (C, C), 1)
        for h in range(GLA_HEADS):
            hl = slice(h * GLA_DK, (h + 1) * GLA_DK)
            vl = slice(h * GLA_DV, (h + 1) * GLA_DV)
            attn = jnp.where(ti == si, _mm_nt(qm[:, hl], km[:, hl]), 0.0)
            for li, w in enumerate(levels):
                p = _mm_nt(qs[li][:, hl], ks[li][:, hl])
                if 2 * w < C:
                    sh = (2 * w).bit_length() - 1
                    p = jnp.where(lax.shift_right_logical(ti, sh) == lax.shift_right_logical(si, sh), p, 0.0)
                attn = attn + p
            vh = v[:, vl].astype(mmdt)
            st = st_ref[h]
            o = _mm(attn.astype(mmdt), vh) + _mm_nt(qb[:, hl], st.astype(mmdt))
            st_ref[h] = st * dec[:, hl] + _mm_tn(vh, kd[:, hl])
            gh = g[:, vl]
            o = ((o * _rms(o)) * gng_ref[...]) * (gh * jax.nn.sigmoid(gh))
            o_ref[rows, vl] = o.astype(o_ref.dtype)
        return carry

    if TT == C:
        chunk(0, 0)
    else:
        lax.fori_loop(0, TT // C, chunk, 0)

    @pl.when(step == pl.num_programs(1) - 1)
    def _():
        for h in range(GLA_HEADS):
            s_out_ref[h] = st_ref[h].T


def _gla(qkvg, la, s0, gng, B, L, row0, TT):
    C = min(GLA_CHUNK, L)
    TT = min(TT, L)
    nT = L // TT
    off = row0 // TT
    exact = C < 16
    m = _gla_sum_matrix(C)
    m = jnp.asarray(m, F32) if exact else jnp.asarray(np.concatenate([m, m, m], axis=1), BF16)
    body = functools.partial(_gla_body, C=C, TT=TT, exact=exact)
    return pl.pallas_call(
        body,
        grid=(B, nT),
        in_specs=[pl.BlockSpec((TT, qkvg.shape[1]), lambda b, i: (off + b * nT + i, 0)),
                  pl.BlockSpec((TT, GLA_KW), lambda b, i: (off + b * nT + i, 0)),
                  pl.BlockSpec((None, GLA_HEADS, GLA_DK, GLA_DV), lambda b, i: (b, 0, 0, 0)),
                  _const_spec(m.shape), _const_spec(gng.shape)],
        out_specs=[pl.BlockSpec((TT, GLA_VW), lambda b, i: (b * nT + i, 0)),
                   pl.BlockSpec((None, GLA_HEADS, GLA_DK, GLA_DV), lambda b, i: (b, 0, 0, 0))],
        out_shape=[jax.ShapeDtypeStruct((B * L, GLA_VW), BF16),
                   jax.ShapeDtypeStruct((B, GLA_HEADS, GLA_DK, GLA_DV), F32)],
        scratch_shapes=[pltpu.VMEM((GLA_HEADS, GLA_DV, GLA_DK), F32)],
        compiler_params=_params("parallel", "arbitrary"),
        name="gla",
    )(qkvg, la, s0, m, gng)


def _conv_body(u_ref, buf_ref, w_ref, cb_ref, lg_ref, lb_ref, o_ref, new_ref, ext_ref, *, TT):
    lo = CONV_HIST - (CONV_K - 1)

    @pl.when(pl.program_id(1) == 0)
    def _():
        ext_ref[lo:CONV_HIST, :] = buf_ref[...]

    ext_ref[CONV_HIST:CONV_HIST + TT, :] = u_ref[...]
    acc = jnp.zeros((TT, CONV_C), F32) + cb_ref[...]
    for j in range(CONV_K):
        acc = acc + w_ref[j:j + 1, :] * ext_ref[lo + j:lo + j + TT, :]
    mu = jnp.mean(acc, axis=-1, keepdims=True)
    d = acc - mu
    var = jnp.mean(d * d, axis=-1, keepdims=True)
    y = (d * lax.rsqrt(var + EPS)) * lg_ref[...] + lb_ref[...]
    o_ref[...] = (y * jax.nn.sigmoid(y)).astype(o_ref.dtype)
    tail = ext_ref[TT + lo:TT + CONV_HIST, :]
    new_ref[...] = tail
    ext_ref[lo:CONV_HIST, :] = tail


def _conv(u, buf, w, cb, lg, lb, B, L, row0, TT):
    TT = min(TT, L)
    nT = L // TT
    off = row0 // TT
    body = functools.partial(_conv_body, TT=TT)
    return pl.pallas_call(
        body,
        grid=(B, nT),
        in_specs=[pl.BlockSpec((TT, CONV_C), lambda b, i: (off + b * nT + i, 0)),
                  pl.BlockSpec((None, CONV_K - 1, CONV_C), lambda b, i: (b, 0, 0)),
                  _const_spec(w.shape), _const_spec(cb.shape), _const_spec(lg.shape), _const_spec(lb.shape)],
        out_specs=[pl.BlockSpec((TT, CONV_C), lambda b, i: (b * nT + i, 0)),
                   pl.BlockSpec((None, CONV_K - 1, CONV_C), lambda b, i: (b, 0, 0))],
        out_shape=[jax.ShapeDtypeStruct((B * L, CONV_C), BF16),
                   jax.ShapeDtypeStruct((B, CONV_K - 1, CONV_C), F32)],
        scratch_shapes=[pltpu.VMEM((CONV_HIST + TT, CONV_C), F32)],
        compiler_params=_params("parallel", "arbitrary"),
        name="conv",
    )(u, buf, w, cb, lg, lb)


def _swa_body(q_ref, kp_ref, kc_ref, vp_ref, vc_ref, sink_ref, o_ref, *, TQ, NB, first_has_history, mmdt):
    for s in range(NB):
        rows = slice(s * TQ, (s + 1) * TQ)
        kp = kp_ref[s] if first_has_history else kp_ref[...]
        vp = vp_ref[s] if first_has_history else vp_ref[...]
        _swa_block(q_ref[rows, :].astype(mmdt), kp.astype(mmdt), kc_ref[rows, :].astype(mmdt), vp.astype(mmdt),
                   vc_ref[rows, :].astype(mmdt), sink_ref, o_ref, rows, TQ=TQ,
                   first_has_history=first_has_history, mmdt=mmdt)


def _swa_block(q, kp, kc, vp, vc, sink_ref, o_ref, rows, *, TQ, first_has_history, mmdt):
    R = SWA_G * TQ
    sh = TQ.bit_length() - 1
    assert TQ == 1 << sh

    def per_row(shape, vals):
        g = lax.shift_right_logical(lax.broadcasted_iota(jnp.int32, shape, 0), sh)
        out = vals[SWA_G - 1]
        for gi in range(SWA_G - 2, -1, -1):
            out = jnp.where(g == gi, vals[gi], out)
        return out

    i1 = lax.broadcasted_iota(jnp.int32, (R, WINDOW), 0) & (TQ - 1)
    j1 = lax.broadcasted_iota(jnp.int32, (R, WINDOW), 1)
    ok1 = j1 >= i1
    if not first_has_history:
        ok1 = jnp.logical_and(ok1, pl.program_id(1) > 0)
    d1 = (WINDOW + i1 - j1).astype(F32)
    i2 = lax.broadcasted_iota(jnp.int32, (R, TQ), 0) & (TQ - 1)
    j2 = lax.broadcasted_iota(jnp.int32, (R, TQ), 1)
    ok2 = j2 <= i2
    d2 = (i2 - j2).astype(F32)
    for kv in range(SWA_HKV):
        heads = [kv * SWA_G + g for g in range(SWA_G)]
        kl = slice(kv * SWA_DH, (kv + 1) * SWA_DH)
        qs = jnp.concatenate([q[:, h * SWA_DH:(h + 1) * SWA_DH] for h in heads], axis=0)
        slopes = [2.0 ** (-8.0 * (h + 1) / SWA_HQ) for h in heads]
        sink = per_row((R, 1), [sink_ref[h] for h in heads])
        s1 = jnp.where(ok1, _mm_nt(qs, kp[:, kl]) - per_row((R, WINDOW), slopes) * d1, -jnp.inf)
        s2 = jnp.where(ok2, _mm_nt(qs, kc[:, kl]) - per_row((R, TQ), slopes) * d2, -jnp.inf)
        m = jnp.maximum(jnp.maximum(jnp.max(s1, -1, keepdims=True), jnp.max(s2, -1, keepdims=True)), sink)
        p1 = jnp.exp(s1 - m)
        p2 = jnp.exp(s2 - m)
        den = jnp.sum(p1, -1, keepdims=True) + jnp.sum(p2, -1, keepdims=True) + jnp.exp(sink - m)
        o = (_mm(p1.astype(mmdt), vp[:, kl]) + _mm(p2.astype(mmdt), vc[:, kl])) / den
        for g, h in enumerate(heads):
            o_ref[rows, h * SWA_DH:(h + 1) * SWA_DH] = o[g * TQ:(g + 1) * TQ].astype(o_ref.dtype)


def _swa(q, k, v, kprev, vprev, sinks, B, L, row0):
    TQ = min(WINDOW, L)
    nT = L // TQ
    off = row0 // TQ
    kvw = SWA_HKV * SWA_DH
    from_start = kprev is None
    NB = 1 if from_start else SWA_SEQS_PER_STEP
    assert B % NB == 0 and row0 % (NB * TQ) == 0
    off = row0 // (NB * TQ)
    cur = pl.BlockSpec((NB * TQ, kvw), lambda b, n: (off + b * nT + n, 0))
    if from_start:
        assert TQ == WINDOW
        prev = pl.BlockSpec((WINDOW, kvw), lambda b, n: (off + b * nT + jnp.maximum(n - 1, 0), 0))
        kprev, vprev = k, v
    else:
        assert L == TQ
        prev = pl.BlockSpec((NB, WINDOW, kvw), lambda b, n: (b, 0, 0))
    body = functools.partial(_swa_body, TQ=TQ, NB=NB, first_has_history=not from_start,
                             mmdt=BF16 if TQ >= 16 else F32)
    return pl.pallas_call(
        body,
        grid=(B // NB, nT),
        in_specs=[pl.BlockSpec((NB * TQ, SWA_HQ * SWA_DH), lambda b, n: (off + b * nT + n, 0)),
                  prev, cur, prev, cur, pl.BlockSpec(memory_space=pltpu.SMEM)],
        out_specs=pl.BlockSpec((NB * TQ, SWA_HQ * SWA_DH), lambda b, n: (b * nT + n, 0)),
        out_shape=jax.ShapeDtypeStruct((B * L, SWA_HQ * SWA_DH), BF16),
        compiler_params=_params("parallel", "arbitrary"),
        name="swa",
    )(q, kprev, k, vprev, v, sinks)


def _route(logits):
    lane = lax.broadcasted_iota(jnp.int32, logits.shape, 1)
    big = jnp.int32(2 * ROUTER_LANES)
    is_g = lane < N_GROUPS
    gl = jnp.where(is_g, logits, -jnp.inf)
    gmax = jnp.max(gl, -1, keepdims=True)
    gsel = jnp.min(jnp.where(gl == gmax, lane, big), -1, keepdims=True)
    gprob = 1.0 / jnp.sum(jnp.where(is_g, jnp.exp(logits - gmax), 0.0), -1, keepdims=True)
    e = lane - EXPERT_LANE0
    in_grp = jnp.logical_and(jnp.logical_and(e >= 0, e < N_EXPERTS),
                             lax.shift_right_logical(jnp.maximum(e, 0), 3) == gsel)
    el = jnp.where(in_grp, logits, -jnp.inf)
    v1 = jnp.max(el, -1, keepdims=True)
    i1 = jnp.min(jnp.where(el == v1, lane, big), -1, keepdims=True)
    el2 = jnp.where(lane == i1, -jnp.inf, el)
    v2 = jnp.max(el2, -1, keepdims=True)
    i2 = jnp.min(jnp.where(el2 == v2, lane, big), -1, keepdims=True)
    t = jnp.exp(v2 - v1)
    w1 = gprob / (1.0 + t)
    w2 = gprob * t / (1.0 + t)
    return lane, i1, i2, w1, w2


def _merge_body(og_ref, oc_ref, os_ref, gates_ref, x_ref, wb_ref, wo_ref, g2_ref, wr_ref, br_ref, tri_ref,
                x1_ref, hf_ref, ridx_ref, rw_ref, cnt_ref, seen_ref):
    @pl.when(pl.program_id(0) == 0)
    def _():
        seen_ref[...] = jnp.zeros_like(seen_ref)

    y = (gates_ref[:, 0:D_MODEL].astype(F32) * _mm(og_ref[...], wb_ref[0])
         + gates_ref[:, D_MODEL:2 * D_MODEL].astype(F32) * _mm(oc_ref[...], wb_ref[1])
         + gates_ref[:, 2 * D_MODEL:].astype(F32) * _mm(os_ref[...], wb_ref[2]))
    x1 = x_ref[...] + _mm(y.astype(BF16), wo_ref[...])
    x1_ref[...] = x1
    hf = (x1 * _rms(x1)) * g2_ref[...]
    hf_ref[...] = hf
    h_hi, h_lo = _split2(hf)
    logits = _mm(h_hi, wr_ref[0]) + (_mm(h_lo, wr_ref[0]) + _mm(h_hi, wr_ref[1])) + br_ref[...]
    lane, i1, i2, w1, w2 = _route(logits)
    onehot = jnp.where(jnp.logical_or(lane == i1, lane == i2), 1.0, 0.0)
    before = _mm(tri_ref[...], onehot.astype(BF16)) + seen_ref[...]
    r1 = jnp.sum(jnp.where(lane == i1, before, 0.0), -1, keepdims=True).astype(jnp.int32)
    r2 = jnp.sum(jnp.where(lane == i2, before, 0.0), -1, keepdims=True).astype(jnp.int32)
    seen = seen_ref[...] + jnp.sum(onehot, axis=0, keepdims=True)
    seen_ref[...] = seen
    cnt_ref[...] = jnp.broadcast_to(seen, cnt_ref.shape)
    ridx_ref[...] = jnp.where(lane == 0, i1 - EXPERT_LANE0,
                              jnp.where(lane == 1, i2 - EXPERT_LANE0,
                                        jnp.where(lane == 2, r1, jnp.where(lane == 3, r2, 0))))
    rw_ref[...] = jnp.where(lane == 0, w1, jnp.where(lane == 1, w2, 0.0))


def _merge(og, oc, osw, gates, x, lw, tm):
    T = x.shape[0]
    row = lambda n: pl.BlockSpec((tm, n), lambda i: (i, 0))
    tri = jnp.asarray(np.tril(np.ones((tm, tm), np.float32), -1), BF16)
    consts = [lw['w_branch'], lw['w_out'], lw['norm_ffn_g'], lw['w_router'], lw['b_router'], tri]
    return pl.pallas_call(
        _merge_body,
        grid=(T // tm,),
        in_specs=[row(BRANCH_W), row(BRANCH_W), row(BRANCH_W), row(3 * D_MODEL), row(D_MODEL)]
                 + [_const_spec(c.shape) for c in consts],
        out_specs=[row(D_MODEL), row(D_MODEL), row(ROUTER_LANES), row(ROUTER_LANES),
                   _const_spec((8, ROUTER_LANES))],
        out_shape=[jax.ShapeDtypeStruct((T, D_MODEL), F32), jax.ShapeDtypeStruct((T, D_MODEL), F32),
                   jax.ShapeDtypeStruct((T, ROUTER_LANES), jnp.int32),
                   jax.ShapeDtypeStruct((T, ROUTER_LANES), F32),
                   jax.ShapeDtypeStruct((8, ROUTER_LANES), F32)],
        scratch_shapes=[pltpu.VMEM((1, ROUTER_LANES), F32)],
        compiler_params=_params("arbitrary"),
        name="merge",
    )(og, oc, osw, gates, x, *consts)


def _moe_tiles(T):
    return -(-TOP_K * T // MOE_TM) + N_EXPERTS


def _dispatch_plan(ridx, cnt, T):
    NT = _moe_tiles(T)
    n_rows = NT * MOE_TM
    counts = cnt[0, EXPERT_LANE0:EXPERT_LANE0 + N_EXPERTS].astype(jnp.int32)
    padded = ((counts + MOE_TM - 1) // MOE_TM) * MOE_TM
    ends = jnp.cumsum(padded)
    start = ends - padded
    tok = jnp.arange(T, dtype=jnp.int32)
    d1 = start[ridx[:, 0]] + ridx[:, 2]
    d2 = start[ridx[:, 1]] + ridx[:, 3]
    row = jnp.arange(n_rows, dtype=jnp.int32)
    spare = TOP_K * T + ((row // MOE_TM) % 2) * MOE_TM + row % MOE_TM
    code = jnp.full((n_rows,), -1, jnp.int32).at[jnp.concatenate([d1, d2])].set(
        jnp.concatenate([2 * tok, 2 * tok + 1]), unique_indices=True)
    src = jnp.maximum(code, 0) // 2
    dst = jnp.where(code < 0, spare, (code % 2) * T + code // 2)
    tile_end = ends // MOE_TM
    n_used = tile_end[-1:]
    tile_expert = jnp.searchsorted(tile_end, jnp.minimum(jnp.arange(NT, dtype=jnp.int32), n_used - 1),
                                   side='right', method='compare_all').astype(jnp.int32)
    return src.reshape(NT, 1, MOE_TM), dst.reshape(NT, 1, MOE_TM), tile_expert, n_used.astype(jnp.int32)


def _ffn_body(te_ref, nu_ref, src_next_ref, src_first_ref, dst_ref, h_hbm, wg_ref, wu_ref, wd_ref, y_hbm,
              xbuf, ybuf, gsem, ssem):
    j = pl.program_id(0)
    last = pl.num_programs(0) - 1
    nu = nu_ref[0]
    slot = lax.rem(j, 2)

    def start_gather(idx_ref, s):
        def body(r, c):
            pltpu.make_async_copy(h_hbm.at[pl.ds(idx_ref[0, 0, r], 1)], xbuf.at[s, pl.ds(r, 1)], gsem.at[s]).start()
            return c
        lax.fori_loop(0, MOE_TM, body, 0, unroll=8)

    def wait_gather(s):
        pltpu.make_async_copy(h_hbm.at[pl.ds(0, MOE_TM)], xbuf.at[s], gsem.at[s]).wait()

    def wait_scatter(s):
        pltpu.make_async_copy(ybuf.at[s], y_hbm.at[pl.ds(0, MOE_TM)], ssem.at[s]).wait()

    @pl.when(j == 0)
    def _():
        start_gather(src_first_ref, 0)
        ybuf[...] = jnp.zeros_like(ybuf)
        spare0 = y_hbm.shape[0] - 2 * MOE_TM
        for s in range(2):
            init = pltpu.make_async_copy(ybuf.at[s], y_hbm.at[pl.ds(spare0 + s * MOE_TM, MOE_TM)], ssem.at[s])
            init.start()
            init.wait()

    @pl.when(jnp.logical_and(j >= 2, j - 2 < nu))
    def _():
        wait_scatter(slot)

    @pl.when(j < nu)
    def _():
        wait_gather(slot)

        @pl.when(j + 1 < nu)
        def _():
            start_gather(src_next_ref, 1 - slot)

        x = xbuf[slot].astype(BF16)
        a = _mm(x, wg_ref[...])
        u = _mm(x, wu_ref[...])
        hid = (a * jax.nn.sigmoid(a)) * u
        ybuf[slot] = _mm(hid.astype(BF16), wd_ref[...])

        def body(r, c):
            pltpu.make_async_copy(ybuf.at[slot, pl.ds(r, 1)], y_hbm.at[pl.ds(dst_ref[0, 0, r], 1)],
                                  ssem.at[slot]).start()
            return c
        lax.fori_loop(0, MOE_TM, body, 0, unroll=8)

    @pl.when(j == last)
    def _():
        @pl.when(jnp.logical_and(j >= 1, j - 1 < nu))
        def _():
            wait_scatter(1 - slot)

        @pl.when(j < nu)
        def _():
            wait_scatter(slot)


def _ffn(hf, plan, lw):
    T = hf.shape[0]
    src, dst, tile_expert, n_used = plan
    NT = src.shape[0]
    idx = lambda f: pl.BlockSpec((1, 1, MOE_TM), f, memory_space=pltpu.SMEM)
    wspec = lambda shape: pl.BlockSpec((None,) + shape, lambda j, te, nu: (te[j], 0, 0))
    grid_spec = pltpu.PrefetchScalarGridSpec(
        num_scalar_prefetch=2,
        grid=(NT,),
        in_specs=[idx(lambda j, te, nu: (jnp.minimum(j + 1, NT - 1), 0, 0)),
                  idx(lambda j, te, nu: (0, 0, 0)),
                  idx(lambda j, te, nu: (j, 0, 0)),
                  pl.BlockSpec(memory_space=pl.ANY),
                  wspec((D_MODEL, EXPERT_F)), wspec((D_MODEL, EXPERT_F)), wspec((EXPERT_F, D_MODEL))],
        out_specs=pl.BlockSpec(memory_space=pl.ANY),
        scratch_shapes=[pltpu.VMEM((2, MOE_TM, D_MODEL), F32), pltpu.VMEM((2, MOE_TM, D_MODEL), F32),
                        pltpu.SemaphoreType.DMA((2,)), pltpu.SemaphoreType.DMA((2,))])
    return pl.pallas_call(
        _ffn_body,
        grid_spec=grid_spec,
        out_shape=jax.ShapeDtypeStruct((TOP_K * T + 2 * MOE_TM, D_MODEL), F32),
        compiler_params=_params("arbitrary"),
        name="ffn",
    )(tile_expert, n_used, src, src, dst, hf, lw['w_exp_gate'], lw['w_exp_up'], lw['w_exp_down'])


def _combine_body(x1_ref, ya_ref, yb_ref, rw_ref, g_ref, o_ref, *, final):
    rw = rw_ref[...]
    lane = lax.broadcasted_iota(jnp.int32, rw.shape, 1)
    w1 = jnp.sum(jnp.where(lane == 0, rw, 0.0), -1, keepdims=True)
    w2 = jnp.sum(jnp.where(lane == 1, rw, 0.0), -1, keepdims=True)
    x2 = x1_ref[...] + (w1 * ya_ref[...] + w2 * yb_ref[...])
    if final:
        x2 = (x2 * _rms(x2)) * g_ref[...]
    o_ref[...] = x2


def _combine(x1, y, rw, g, row0, rows, tm, final):
    T = x1.shape[0]
    off = row0 // tm
    body = functools.partial(_combine_body, final=final)
    return pl.pallas_call(
        body,
        grid=(rows // tm,),
        in_specs=[pl.BlockSpec((tm, D_MODEL), lambda i: (off + i, 0)),
                  pl.BlockSpec((tm, D_MODEL), lambda i: (off + i, 0)),
                  pl.BlockSpec((tm, D_MODEL), lambda i: (T // tm + off + i, 0)),
                  pl.BlockSpec((tm, ROUTER_LANES), lambda i: (off + i, 0)),
                  _const_spec(g.shape)],
        out_specs=pl.BlockSpec((tm, D_MODEL), lambda i: (i, 0)),
        out_shape=jax.ShapeDtypeStruct((rows, D_MODEL), F32),
        compiler_params=_params("parallel"),
        name="combine",
    )(x1, y, y, rw, g)


def _prep_layer(l, norm_mix_g, w_in, w_fgate2, b_fgate2, gla_norm_g, conv_w, conv_b, conv_ln_g, conv_ln_b,
                swa_sinks, w_branch, w_out, norm_ffn_g, w_router_group, b_router_group, w_router_expert,
                b_router_expert, w_exp_gate, w_exp_up, w_exp_down):
    w = w_in[l]
    c0 = 2 * GLA_KW + 2 * GLA_VW
    c1 = c0 + GATE_RANK
    c2 = c1 + 2 * CONV_C
    c3 = c2 + SWA_HQ * SWA_DH + 2 * SWA_HKV * SWA_DH
    pad_r = LANES - GATE_RANK
    wf2 = jnp.pad(w_fgate2[l], ((0, pad_r), (0, 0)))
    wr = jnp.pad(jnp.concatenate([w_router_group[l], w_router_expert[l]], axis=1),
                 ((0, 0), (0, ROUTER_LANES - N_GROUPS - N_EXPERTS)))
    br = jnp.pad(jnp.concatenate([b_router_group[l], b_router_expert[l]]),
                 (0, ROUTER_LANES - N_GROUPS - N_EXPERTS))
    return {
        'norm_mix_g': norm_mix_g[l][None, :],
        'w_qkvg': w[:, :c0].astype(BF16),
        'w_fr': jnp.pad(w[:, c0:c1], ((0, 0), (0, pad_r))).astype(BF16),
        'w_f2': jnp.stack(_split2(wf2)),
        'b_f2': b_fgate2[l][None, :],
        'w_cglu': w[:, c1:c2].astype(BF16),
        'w_swa': w[:, c2:c3].astype(BF16),
        'w_mg': w[:, c3:].astype(BF16),
        'gla_norm_g': gla_norm_g[l][None, :],
        'conv_w': conv_w[l],
        'conv_b': conv_b[l][None, :],
        'conv_ln_g': conv_ln_g[l][None, :],
        'conv_ln_b': conv_ln_b[l][None, :],
        'swa_sinks': swa_sinks[l],
        'w_branch': w_branch[l].reshape(3, BRANCH_W, D_MODEL).astype(BF16),
        'w_out': w_out[l].astype(BF16),
        'norm_ffn_g': norm_ffn_g[l][None, :],
        'w_router': jnp.stack(_split2(wr)),
        'b_router': br[None, :],
        'w_exp_gate': w_exp_gate[l].astype(BF16),
        'w_exp_up': w_exp_up[l].astype(BF16),
        'w_exp_down': w_exp_down[l].astype(BF16),
    }


def kernel(x_prompt, x_sample, state_gla, cache_conv, cache_swa_k, cache_swa_v, norm_mix_g, w_in, w_fgate2, b_fgate2, gla_norm_g, conv_w, conv_b, conv_ln_g, conv_ln_b, swa_sinks, w_branch, w_out, norm_ffn_g, w_router_group, b_router_group, w_router_expert, b_router_expert, w_exp_gate, w_exp_up, w_exp_down, final_norm_g):
    depth = w_in.shape[0]
    Bp, Lp, _ = x_prompt.shape
    Bs, Ls, _ = x_sample.shape
    Tp, Ts = Bp * Lp, Bs * Ls
    T = Tp + Ts
    nbuf = cache_swa_k.shape[2]
    assert nbuf == WINDOW and PAST_LEN >= nbuf
    kvw = SWA_HKV * SWA_DH
    x = jnp.concatenate([x_prompt.reshape(Tp, D_MODEL), x_sample.reshape(Ts, D_MODEL)], axis=0)
    gla_p, gla_s, conv_p, conv_s, kp, ks_, vp, vs = [], [], [], [], [], [], [], []
    keep = min(WINDOW, Lp)
    g_final = final_norm_g[None, :]
    for l in range(depth):
        lw = _prep_layer(l, norm_mix_g, w_in, w_fgate2, b_fgate2, gla_norm_g, conv_w, conv_b, conv_ln_g,
                         conv_ln_b, swa_sinks, w_branch, w_out, norm_ffn_g, w_router_group, b_router_group,
                         w_router_expert, b_router_expert, w_exp_gate, w_exp_up, w_exp_down)
        qkvg, la, u, sq, sk, sv, gates = _in_proj(x, lw, TOKEN_TILE)
        s0 = jnp.zeros((Bp, GLA_HEADS, GLA_DK, GLA_DV), F32)
        cb0 = jnp.zeros((Bp, CONV_K - 1, CONV_C), F32)
        og_p, a1 = _gla(qkvg, la, s0, lw['gla_norm_g'], Bp, Lp, 0, SEQ_TILE)
        oc_p, a2 = _conv(u, cb0, lw['conv_w'], lw['conv_b'], lw['conv_ln_g'], lw['conv_ln_b'], Bp, Lp, 0, SEQ_TILE)
        os_p = _swa(sq, sk, sv, None, None, lw['swa_sinks'], Bp, Lp, 0)
        ck = cache_swa_k[l].reshape(Bs, nbuf, kvw)
        cv = cache_swa_v[l].reshape(Bs, nbuf, kvw)
        og_s, b1 = _gla(qkvg, la, state_gla[l], lw['gla_norm_g'], Bs, Ls, Tp, SEQ_TILE)
        oc_s, b2 = _conv(u, cache_conv[l], lw['conv_w'], lw['conv_b'], lw['conv_ln_g'], lw['conv_ln_b'],
                         Bs, Ls, Tp, SEQ_TILE)
        os_s = _swa(sq, sk, sv, ck, cv, lw['swa_sinks'], Bs, Ls, Tp)
        cat = lambda a, b: jnp.concatenate([a, b], axis=0)
        x1, hf, ridx, rw, cnt = _merge(cat(og_p, og_s), cat(oc_p, oc_s), cat(os_p, os_s), gates, x, lw, TOKEN_TILE)
        y = _ffn(hf, _dispatch_plan(ridx, cnt, T), lw)
        if l + 1 < depth:
            x = _combine(x1, y, rw, g_final, 0, T, TOKEN_TILE, False)
        else:
            y_prompt = _combine(x1, y, rw, g_final, 0, Tp, TOKEN_TILE, True).reshape(Bp, Lp, D_MODEL)
            y_sample = _combine(x1, y, rw, g_final, Tp, Ts, TOKEN_TILE, True).reshape(Bs, Ls, D_MODEL)
        gla_p.append(a1)
        gla_s.append(b1)
        conv_p.append(a2)
        conv_s.append(b2)
        k_p = sk[:Tp].reshape(Bp, Lp, kvw)
        v_p = sv[:Tp].reshape(Bp, Lp, kvw)
        kp.append(k_p[:, -keep:].reshape(Bp, keep, SWA_HKV, SWA_DH))
        vp.append(v_p[:, -keep:].reshape(Bp, keep, SWA_HKV, SWA_DH))
        k_s = sk[Tp:].reshape(Bs, Ls, kvw)
        v_s = sv[Tp:].reshape(Bs, Ls, kvw)
        ks_.append(jnp.concatenate([ck, k_s], axis=1)[:, -nbuf:].reshape(Bs, nbuf, SWA_HKV, SWA_DH))
        vs.append(jnp.concatenate([cv, v_s], axis=1)[:, -nbuf:].reshape(Bs, nbuf, SWA_HKV, SWA_DH))
    return (y_prompt, y_sample, jnp.stack(gla_p), jnp.stack(gla_s), jnp.stack(conv_p), jnp.stack(conv_s),
            jnp.stack(kp), jnp.stack(ks_), jnp.stack(vp), jnp.stack(vs))
```
